```python
import jax, jax.numpy as jnp
from jax import lax
import numpy as np

D_MODEL = 2048
BATCH = 2
SEQ = 4096
DEPTH = 4
DEC_BATCH = 8
DEC_SEQ = 1
PAST_LEN = 16384
PAGE_SIZE = 128

N_HEADS = 8
HEAD_DIM = 128
ATTN_W = N_HEADS * HEAD_DIM
ROT_DIM = HEAD_DIM // 4
ROPE_THETA = 500000.0
N_IDX_HEADS = 16
IDX_HEAD_DIM = 64
IDX_ROT_DIM = IDX_HEAD_DIM // 4
TOPK_MAX = 256
Q_BLOCK = 128
POOL_WINDOWS = (2, 4, 8, 16)
N_POOL_GROUPS = 4
POOL_GROUP_W = 256
POOL_W = N_POOL_GROUPS * POOL_GROUP_W
POOL_MAXW = max(POOL_WINDOWS)
POOL_HIST = POOL_MAXW - 1
D_FF = 5632
N_BRANCHES = 2
NORM_EPS = 1e-6
IN_SPLITS = (POOL_W, ATTN_W, ATTN_W, ATTN_W, N_IDX_HEADS * IDX_HEAD_DIM, IDX_HEAD_DIM, N_IDX_HEADS, N_BRANCHES * D_MODEL)
D_IN = sum(IN_SPLITS)

kernel_name = 'hybrid_pool_dsa_macaron_step'


def rms_norm(x, g):
    x32 = x.astype(jnp.float32)
    y = x32 * lax.rsqrt(jnp.mean(x32 * x32, axis=-1, keepdims=True) + NORM_EPS)
    return (y * g.astype(jnp.float32)).astype(x.dtype)


def swiglu(x, w1, w3, w2):
    return (jax.nn.silu(x @ w1) * (x @ w3)) @ w2


def partial_rotary(x, pos, rot_dim):
    half = rot_dim // 2
    inv_freq = jnp.float32(ROPE_THETA) ** (-jnp.arange(half, dtype=jnp.float32) * (2.0 / rot_dim))
    ang = pos[:, None] * inv_freq[None, :]
    cos = jnp.cos(ang)[:, None, :]
    sin = jnp.sin(ang)[:, None, :]
    x32 = x.astype(jnp.float32)
    x1 = x32[..., :half]
    x2 = x32[..., half:rot_dim]
    out = jnp.concatenate([x1 * cos - x2 * sin, x1 * sin + x2 * cos, x32[..., rot_dim:]], axis=-1)
    return out.astype(x.dtype)


def split_in(z):
    offs, acc = [], 0
    for w in IN_SPLITS[:-1]:
        acc += w
        offs.append(acc)
    return jnp.split(z, offs, axis=-1)


def mixer_project(h, w_in_l, pos):
    B, T, _ = h.shape
    u, q, k, v, qi, ki, wi, gl = split_in(h @ w_in_l)
    q = partial_rotary(q.reshape(B, T, N_HEADS, HEAD_DIM), pos, ROT_DIM)
    k = partial_rotary(k.reshape(B, T, N_HEADS, HEAD_DIM), pos, ROT_DIM)
    v = v.reshape(B, T, N_HEADS, HEAD_DIM)
    qi = partial_rotary(qi.reshape(B, T, N_IDX_HEADS, IDX_HEAD_DIM), pos, IDX_ROT_DIM)
    ki = partial_rotary(ki[:, :, None, :], pos, IDX_ROT_DIM)[:, :, 0, :]
    wi = wi * (N_IDX_HEADS * IDX_HEAD_DIM) ** -0.5
    return u, q, k, v, qi, ki, wi, gl


def pool_mix(u_hist, u_new, pos, w_group, scale):
    B, T, _ = u_new.shape
    if u_hist is None:
        P, seq = 0, u_new
    else:
        P, seq = u_hist.shape[1], jnp.concatenate([u_hist, u_new], axis=1)
    c = jnp.cumsum(seq.astype(jnp.float32), axis=1)
    c = jnp.concatenate([jnp.zeros((B, POOL_MAXW, POOL_W), jnp.float32), c], axis=1)
    u32 = u_new.astype(jnp.float32)
    outs = []
    for g, w in enumerate(POOL_WINDOWS):
        lo, hi = g * POOL_GROUP_W, (g + 1) * POOL_GROUP_W
        s = c[:, POOL_MAXW + P:POOL_MAXW + P + T, lo:hi] - c[:, POOL_MAXW + P - w:POOL_MAXW + P - w + T, lo:hi]
        cnt = jnp.minimum(pos + 1.0, float(w))[None, :, None]
        outs.append(s / cnt - u32[:, :, lo:hi])
    d = jnp.stack(outs, axis=2).astype(u_new.dtype)
    y = jnp.einsum('btgc,gcd->btgd', d, w_group).reshape(B, T, POOL_W)
    return y * scale, seq[:, -POOL_HIST:]


def indexer_topk(qi, wi, ki, q_pos, topk):
    s = jnp.einsum('bthd,bsd->bths', qi, ki)
    score = jnp.einsum('bths,bth->bts', jax.nn.relu(s), wi).astype(jnp.float32)
    key_pos = jnp.arange(ki.shape[1], dtype=jnp.int32)
    causal = key_pos[None, :] <= q_pos[:, None]
    score = jnp.where(causal[None], score, -jnp.inf)
    _, idx = lax.top_k(score, topk)
    valid = idx <= q_pos[None, :, None]
    return idx, valid


def sparse_attend(q, k_sel, v_sel, valid):
    s = jnp.einsum('bthd,btkhd->bthk', q, k_sel).astype(jnp.float32) * (HEAD_DIM ** -0.5)
    s = jnp.where(valid[:, :, None, :], s, -jnp.inf)
    p = jax.nn.softmax(s, axis=-1).astype(v_sel.dtype)
    return jnp.einsum('bthk,btkhd->bthd', p, v_sel)


gather_rows = jax.vmap(lambda rows, idx: rows[idx])


def prompt_dsa(q, k, v, qi, ki, wi, topk):
    B, S = q.shape[:2]
    nb = S // Q_BLOCK

    def to_blocks(a):
        return a.reshape((B, nb, Q_BLOCK) + a.shape[2:]).swapaxes(0, 1)

    def block(args):
        qb, qib, wib, t0 = args
        q_pos = t0 + jnp.arange(Q_BLOCK, dtype=jnp.int32)
        idx, valid = indexer_topk(qib, wib, ki, q_pos, topk)
        return sparse_attend(qb, gather_rows(k, idx), gather_rows(v, idx), valid)

    t0s = jnp.arange(nb, dtype=jnp.int32) * Q_BLOCK
    out = lax.map(block, (to_blocks(q), to_blocks(qi), to_blocks(wi), t0s))
    return out.swapaxes(0, 1).reshape(B, S, N_HEADS, HEAD_DIM)


def sample_dsa(q, k_new, v_new, qi, ki_new, wi, cache_k, cache_v, cache_kidx, layer, page_table, topk):
    DB, T = q.shape[:2]
    n_pages = PAST_LEN // PAGE_SIZE
    past = n_pages * PAGE_SIZE
    ki_past = cache_kidx[layer, page_table].reshape(DB, past, IDX_HEAD_DIM).astype(ki_new.dtype)
    ki_all = jnp.concatenate([ki_past, ki_new], axis=1)
    q_pos = past + jnp.arange(T, dtype=jnp.int32)
    idx, valid = indexer_topk(qi, wi, ki_all, q_pos, topk)
    is_new = (idx >= past)[..., None, None]
    pidx = jnp.minimum(idx, past - 1)
    phys = gather_rows(page_table, pidx // PAGE_SIZE)
    off = pidx % PAGE_SIZE
    nidx = jnp.clip(idx - past, 0, T - 1)
    k_sel = jnp.where(is_new, gather_rows(k_new, nidx), cache_k[layer, phys, off].astype(k_new.dtype))
    v_sel = jnp.where(is_new, gather_rows(v_new, nidx), cache_v[layer, phys, off].astype(v_new.dtype))
    return sparse_attend(q, k_sel, v_sel, valid)


def merge_out(pool_y, attn_o, gl, w_up_pool_l, w_up_attn_l, w_out_l):
    B, T = pool_y.shape[:2]
    g_pool, g_attn = jnp.split(gl, N_BRANCHES, axis=-1)
    m = (jax.nn.sigmoid(g_pool) * (pool_y @ w_up_pool_l)
         + jax.nn.sigmoid(g_attn) * (attn_o.reshape(B, T, ATTN_W) @ w_up_attn_l))
    return m @ w_out_l


def setup_inputs(seed: int = 0) -> dict:
    key = jax.random.key(seed)
    ks = jax.random.split(key, 32)
    f32 = jnp.float32
    n_pages = PAST_LEN // PAGE_SIZE
    n_used = DEC_BATCH * n_pages
    n_pool = n_used + max(1, n_used // 4)

    def nrm(k, shape, scale=1.0):
        return jax.random.normal(k, shape, f32) * scale

    def gain(k, shape):
        return 1.0 + 0.02 * jax.random.normal(k, shape, f32)

    page_table = jax.random.permutation(ks[7], n_pool)[:n_used].reshape(DEC_BATCH, n_pages).astype(jnp.int32)
    return {
        'x_prompt': nrm(ks[0], (BATCH, SEQ, D_MODEL)),
        'x_sample': nrm(ks[1], (DEC_BATCH, DEC_SEQ, D_MODEL)),
        'cache_k': nrm(ks[2], (DEPTH, n_pool, PAGE_SIZE, N_HEADS, HEAD_DIM)),
        'cache_v': nrm(ks[3], (DEPTH, n_pool, PAGE_SIZE, N_HEADS, HEAD_DIM)),
        'cache_kidx': nrm(ks[4], (DEPTH, n_pool, PAGE_SIZE, IDX_HEAD_DIM)),
        'state_pool': nrm(ks[5], (DEPTH, DEC_BATCH, POOL_HIST, POOL_W)),
        'page_table': page_table,
        'ffn1_norm': gain(ks[8], (DEPTH, D_MODEL)),
        'ffn1_w1': nrm(ks[9], (DEPTH, D_MODEL, D_FF), D_MODEL ** -0.5),
        'ffn1_w3': nrm(ks[10], (DEPTH, D_MODEL, D_FF), D_MODEL ** -0.5),
        'ffn1_w2': nrm(ks[11], (DEPTH, D_FF, D_MODEL), D_FF ** -0.5),
        'mix_norm': gain(ks[12], (DEPTH, D_MODEL)),
        'w_in': nrm(ks[13], (DEPTH, D_MODEL, D_IN), D_MODEL ** -0.5),
        'pool_group_w': nrm(ks[14], (DEPTH, N_POOL_GROUPS, POOL_GROUP_W, POOL_GROUP_W), POOL_GROUP_W ** -0.5),
        'pool_scale': gain(ks[15], (DEPTH, POOL_W)),
        'w_up_pool': nrm(ks[16], (DEPTH, POOL_W, D_MODEL), POOL_W ** -0.5),
        'w_up_attn': nrm(ks[17], (DEPTH, ATTN_W, D_MODEL), ATTN_W ** -0.5),
        'w_out': nrm(ks[18], (DEPTH, D_MODEL, D_MODEL), D_MODEL ** -0.5),
        'ffn2_norm': gain(ks[19], (DEPTH, D_MODEL)),
        'ffn2_w1': nrm(ks[20], (DEPTH, D_MODEL, D_FF), D_MODEL ** -0.5),
        'ffn2_w3': nrm(ks[21], (DEPTH, D_MODEL, D_FF), D_MODEL ** -0.5),
        'ffn2_w2': nrm(ks[22], (DEPTH, D_FF, D_MODEL), D_FF ** -0.5),
        'final_norm': gain(ks[23], (D_MODEL,)),
    }


def reference(x_prompt, x_sample, cache_k, cache_v, cache_kidx, state_pool, page_table,
              ffn1_norm, ffn1_w1, ffn1_w3, ffn1_w2, mix_norm, w_in, pool_group_w, pool_scale,
              w_up_pool, w_up_attn, w_out, ffn2_norm, ffn2_w1, ffn2_w3, ffn2_w2, final_norm):
    S_p = x_prompt.shape[1]
    T_s = x_sample.shape[1]
    pos_p = jnp.arange(S_p, dtype=jnp.float32)
    pos_s = PAST_LEN + jnp.arange(T_s, dtype=jnp.float32)
    topk_p = min(TOPK_MAX, S_p // 4)
    topk_s = min(TOPK_MAX, (PAST_LEN + T_s) // 4)

    xp, xs = x_prompt, x_sample
    kp, vp, kip, pp, ks_, vs_, kis, ps = [], [], [], [], [], [], [], []
    for l in range(DEPTH):
        xp = xp + 0.5 * swiglu(rms_norm(xp, ffn1_norm[l]), ffn1_w1[l], ffn1_w3[l], ffn1_w2[l])
        xs = xs + 0.5 * swiglu(rms_norm(xs, ffn1_norm[l]), ffn1_w1[l], ffn1_w3[l], ffn1_w2[l])

        u, q, k, v, qi, ki, wi, gl = mixer_project(rms_norm(xp, mix_norm[l]), w_in[l], pos_p)
        pool_y, hist = pool_mix(None, u, pos_p, pool_group_w[l], pool_scale[l])
        attn_o = prompt_dsa(q, k, v, qi, ki, wi, topk_p)
        xp = xp + merge_out(pool_y, attn_o, gl, w_up_pool[l], w_up_attn[l], w_out[l])
        kp.append(k); vp.append(v); kip.append(ki); pp.append(hist)

        u, q, k, v, qi, ki, wi, gl = mixer_project(rms_norm(xs, mix_norm[l]), w_in[l], pos_s)
        pool_y, hist = pool_mix(state_pool[l].astype(u.dtype), u, pos_s, pool_group_w[l], pool_scale[l])
        attn_o = sample_dsa(q, k, v, qi, ki, wi, cache_k, cache_v, cache_kidx, l, page_table, topk_s)
        xs = xs + merge_out(pool_y, attn_o, gl, w_up_pool[l], w_up_attn[l], w_out[l])
        ks_.append(k); vs_.append(v); kis.append(ki); ps.append(hist)

        xp = xp + 0.5 * swiglu(rms_norm(xp, ffn2_norm[l]), ffn2_w1[l], ffn2_w3[l], ffn2_w2[l])
        xs = xs + 0.5 * swiglu(rms_norm(xs, ffn2_norm[l]), ffn2_w1[l], ffn2_w3[l], ffn2_w2[l])

    y_prompt = rms_norm(xp, final_norm)
    y_sample = rms_norm(xs, final_norm)
    return (y_prompt, y_sample, jnp.stack(kp), jnp.stack(vp), jnp.stack(kip), jnp.stack(pp),
            jnp.stack(ks_), jnp.stack(vs_), jnp.stack(kis), jnp.stack(ps))
```

```python
import functools

import jax
import jax.numpy as jnp
from jax import lax
from jax.experimental import pallas as pl
from jax.experimental.pallas import tpu as pltpu

F32 = jnp.float32
BF16 = jnp.bfloat16
I32 = jnp.int32

D_MODEL = 2048
DEPTH = 4
PAST_LEN = 16384
PAGE_SIZE = 128
N_PAGES = PAST_LEN // PAGE_SIZE
N_HEADS = 8
HEAD_DIM = 128
ATTN_W = N_HEADS * HEAD_DIM
ROT_DIM = HEAD_DIM // 4
ROPE_THETA = 500000.0
N_IDX_HEADS = 16
IDX_HEAD_DIM = 64
IDX_W = N_IDX_HEADS * IDX_HEAD_DIM
IDX_ROT_DIM = IDX_HEAD_DIM // 4
TOPK_MAX = 256
POOL_WINDOWS = (2, 4, 8, 16)
POOL_GROUP_W = 256
POOL_W = len(POOL_WINDOWS) * POOL_GROUP_W
POOL_HIST = max(POOL_WINDOWS) - 1
D_FF = 5632
NORM_EPS = 1e-6

LANES = 128
HALO = 16
VMEM_LIMIT = 56 * 1024 * 1024
INT_MIN = -2 ** 31
NEG_BIG = -1e30
SAMPLE_ROWS = 16

COL_U, COL_Q, COL_K, COL_V, COL_QI = 0, 1024, 2048, 3072, 4096
COL_GL = 5120
COL_KIWI = COL_GL + 2 * D_MODEL
D_IN_PAD = COL_KIWI + LANES


def _params(*sem):
    return pltpu.CompilerParams(dimension_semantics=sem, vmem_limit_bytes=VMEM_LIMIT)


def _rms(x, g):
    y = x * lax.rsqrt(jnp.mean(x * x, axis=-1, keepdims=True) + NORM_EPS)
    return y * g


def _dot(a, b):
    return jnp.dot(a, b, preferred_element_type=F32)


def _dot_nt(a, b):
    return lax.dot_general(a, b, (((1,), (1,)), ((), ())), preferred_element_type=F32)


def _ffn_kernel(x_ref, g_ref, w1_ref, w3_ref, w2_ref, g2_ref, o_ref, n_ref, xn_ref, acc_ref):
    f = pl.program_id(1)

    @pl.when(f == 0)
    def _():
        xn_ref[...] = _rms(x_ref[...], g_ref[...]).astype(BF16)
        acc_ref[...] = jnp.zeros_like(acc_ref)

    xn = xn_ref[...]
    a = _dot(xn, w1_ref[...])
    b = _dot(xn, w3_ref[...])
    h = (a * jax.nn.sigmoid(a)) * b
    acc_ref[...] += _dot(h.astype(BF16), w2_ref[...])

    @pl.when(f == pl.num_programs(1) - 1)
    def _():
        out = x_ref[...] + 0.5 * acc_ref[...]
        o_ref[...] = out
        n_ref[...] = _rms(out, g2_ref[...]).astype(n_ref.dtype)


def ffn(x, g, w1, w3, w2, g2, layer, *, norm_dtype, tm, tf):
    m, d = x.shape
    d_ff = w1.shape[2]
    return pl.pallas_call(
        _ffn_kernel,
        grid=(m // tm, d_ff // tf),
        in_specs=[
            pl.BlockSpec((tm, d), lambda i, f: (i, 0)),
            pl.BlockSpec((1, d), lambda i, f: (0, 0)),
            pl.BlockSpec((None, d, tf), lambda i, f: (layer, 0, f)),
            pl.BlockSpec((None, d, tf), lambda i, f: (layer, 0, f)),
            pl.BlockSpec((None, tf, d), lambda i, f: (layer, f, 0)),
            pl.BlockSpec((1, d), lambda i, f: (0, 0)),
        ],
        out_specs=[
            pl.BlockSpec((tm, d), lambda i, f: (i, 0)),
            pl.BlockSpec((tm, d), lambda i, f: (i, 0)),
        ],
        out_shape=[jax.ShapeDtypeStruct((m, d), F32), jax.ShapeDtypeStruct((m, d), norm_dtype)],
        scratch_shapes=[pltpu.VMEM((tm, d), BF16), pltpu.VMEM((tm, d), F32)],
        compiler_params=_params("parallel", "arbitrary"),
        name="ffn",
    )(x, g, w1, w3, w2, g2)


def _rotate(z, c, s_up, s_dn, half):
    outs = []
    for g in range(z.shape[1] // LANES):
        zg = z[:, g * LANES:(g + 1) * LANES]
        up = pltpu.roll(zg, LANES - half, 1)
        dn = pltpu.roll(zg, half, 1)
        outs.append(zg * c + up * s_up + dn * s_dn)
    return outs[0] if len(outs) == 1 else jnp.concatenate(outs, axis=1)


def _proj_kernel(*refs, rot_half, n_out, dup):
    if rot_half:
        a_ref, w_ref, c_ref, su_ref, sd_ref = refs[:5]
        outs = refs[5:]
    else:
        a_ref, w_ref = refs[:2]
        outs = refs[2:]
    z = _dot(a_ref[...], w_ref[...])
    if rot_half:
        z = _rotate(z, c_ref[...], su_ref[...], sd_ref[...], rot_half)
    for o_ref in outs[:n_out]:
        o_ref[...] = z.astype(o_ref.dtype)
    if dup:
        lane = lax.broadcasted_iota(I32, z.shape, 1)
        zz = jnp.where(lane < IDX_HEAD_DIM, z, pltpu.roll(z, IDX_HEAD_DIM, 1))
        outs[n_out][...] = zz.astype(outs[n_out].dtype)


def proj(a, w, layer, col, n, out_dtypes, *, tm, tn, tables=None, rot_half=0, seq=None, dup=False):
    m, k = a.shape
    tn = min(tn, n)
    cb = col // tn
    in_specs = [
        pl.BlockSpec((tm, k), lambda i, j: (i, 0)),
        pl.BlockSpec((None, k, tn), lambda i, j: (layer, 0, cb + j)),
    ]
    args = [a, w]
    if rot_half:
        nt = seq // tm
        tspec = pl.BlockSpec((tm, LANES), lambda i, j: (i % nt, 0))
        in_specs += [tspec, tspec, tspec]
        args += list(tables)
    dts = list(out_dtypes) + ([BF16] if dup else [])
    return pl.pallas_call(
        functools.partial(_proj_kernel, rot_half=rot_half, n_out=len(out_dtypes), dup=dup),
        grid=(m // tm, n // tn),
        in_specs=in_specs,
        out_specs=[pl.BlockSpec((tm, tn), lambda i, j: (i, j)) for _ in dts],
        out_shape=[jax.ShapeDtypeStruct((m, n), dt) for dt in dts],
        compiler_params=_params("parallel", "arbitrary"),
        name="proj",
    )(*args)


def _pool_groups(buf_ref, u, cnt_fn, wg_ref, sc_ref, o_ref, tm):
    for g, w in enumerate(POOL_WINDOWS):
        lo, hi = g * POOL_GROUP_W, (g + 1) * POOL_GROUP_W
        s = buf_ref[HALO:HALO + tm, lo:hi]
        for j in range(1, w):
            s = s + buf_ref[HALO - j:HALO - j + tm, lo:hi]
        d = s / cnt_fn(w) - u[:, lo:hi]
        y = _dot(d.astype(BF16), wg_ref[g]) * sc_ref[:, lo:hi]
        o_ref[:, lo:hi] = y.astype(o_ref.dtype)


def _pool_prompt_kernel(u_ref, prev_ref, wg_ref, sc_ref, o_ref, buf_ref, *, tm):
    i = pl.program_id(1)
    u = u_ref[...]
    buf_ref[0:HALO, :] = jnp.where(i == 0, 0.0, prev_ref[...])
    buf_ref[HALO:HALO + tm, :] = u
    pos = (i * tm + lax.broadcasted_iota(I32, (tm, 1), 0)).astype(F32)
    _pool_groups(buf_ref, u, lambda w: jnp.minimum(pos + 1.0, float(w)), wg_ref, sc_ref, o_ref, tm)


def pool_prompt(u, wg, sc, layer, *, batch, seq, tm):
    nt = seq // tm
    hb = tm // HALO
    return pl.pallas_call(
        functools.partial(_pool_prompt_kernel, tm=tm),
        grid=(batch, nt),
        in_specs=[
            pl.BlockSpec((tm, POOL_W), lambda b, i: (b * nt + i, 0)),
            pl.BlockSpec((HALO, POOL_W), lambda b, i: (jnp.maximum((b * nt + i) * hb - 1, 0), 0)),
            pl.BlockSpec((None, len(POOL_WINDOWS), POOL_GROUP_W, POOL_GROUP_W), lambda b, i: (layer, 0, 0, 0)),
            pl.BlockSpec((1, POOL_W), lambda b, i: (0, 0)),
        ],
        out_specs=pl.BlockSpec((tm, POOL_W), lambda b, i: (b * nt + i, 0)),
        out_shape=jax.ShapeDtypeStruct((batch * seq, POOL_W), BF16),
        scratch_shapes=[pltpu.VMEM((HALO + tm, POOL_W), F32)],
        compiler_params=_params("parallel", "arbitrary"),
        name="pool_prompt",
    )(u, u, wg, sc)


def _pool_sample_kernel(u_ref, st_ref, wg_ref, sc_ref, o_ref):
    u = u_ref[...]
    for g, w in enumerate(POOL_WINDOWS):
        lo, hi = g * POOL_GROUP_W, (g + 1) * POOL_GROUP_W
        s = u[:, lo:hi]
        for j in range(1, w):
            s = s + st_ref[POOL_HIST - j, :, lo:hi]
        d = s / float(w) - u[:, lo:hi]
        y = _dot(d.astype(BF16), wg_ref[g]) * sc_ref[:, lo:hi]
        o_ref[:, lo:hi] = y.astype(o_ref.dtype)


def pool_sample(u, state_t, wg, sc, layer):
    rows = u.shape[0]
    return pl.pallas_call(
        _pool_sample_kernel,
        grid=(1,),
        in_specs=[
            pl.BlockSpec((rows, POOL_W), lambda i: (0, 0)),
            pl.BlockSpec((POOL_HIST, rows, POOL_W), lambda i: (0, 0, 0)),
            pl.BlockSpec((None, len(POOL_WINDOWS), POOL_GROUP_W, POOL_GROUP_W), lambda i: (layer, 0, 0, 0)),
            pl.BlockSpec((1, POOL_W), lambda i: (0, 0)),
        ],
        out_specs=pl.BlockSpec((rows, POOL_W), lambda i: (0, 0)),
        out_shape=jax.ShapeDtypeStruct((rows, POOL_W), BF16),
        compiler_params=_params("arbitrary"),
        name="pool_sample",
    )(u, state_t, wg, sc)


def _sortable(score):
    bits = lax.bitcast_convert_type(score, I32)
    return jnp.where(bits >= 0, bits, bits ^ jnp.int32(0x7FFFFFFF))


def _search_bits(count_fn, rows, nbits, accept):
    def step(i, acc):
        bit = jnp.left_shift(jnp.int32(1), (nbits - 1) - i)
        cand = acc | bit
        return jnp.where(accept(count_fn(cand)), cand, acc)
    return lax.fori_loop(0, nbits, step, jnp.zeros((rows, 1), I32))


def _dsa_prompt_kernel(qi_ref, kw_ref, kd_ref, q_ref, k_ref, v_ref, o_ref,
                       qm_ref, key_ref, bias_ref, *, tq, tk, topk, pos_bits):
    qb = pl.program_id(1)
    n_chunks = (qb * tq + tq + tk - 1) // tk
    q_pos = qb * tq + lax.broadcasted_iota(I32, (tq, 1), 0)
    lane = lax.broadcasted_iota(I32, (tq, LANES), 1)

    for hp in range(N_IDX_HEADS // 2):
        pair = qi_ref[:, hp * LANES:(hp + 1) * LANES]
        qm_ref[2 * hp] = jnp.where(lane < IDX_HEAD_DIM, pair, jnp.zeros_like(pair))
        qm_ref[2 * hp + 1] = jnp.where(lane >= IDX_HEAD_DIM, pair, jnp.zeros_like(pair))
    wi = kw_ref[:, IDX_HEAD_DIM:IDX_HEAD_DIM + N_IDX_HEADS]

    def key_pos(c):
        return c * tk + lax.broadcasted_iota(I32, (1, tk), 1)

    def chunk(c):
        return pl.ds(pl.multiple_of(c * tk, tk), tk)

    def score_chunk(c, carry):
        kc = kd_ref[chunk(c), :]
        acc = jnp.zeros((tq, tk), F32)
        for h in range(N_IDX_HEADS):
            s = _dot_nt(qm_ref[h], kc)
            acc = acc + jnp.maximum(s, 0.0) * wi[:, h:h + 1]
        keys = jnp.where(key_pos(c) <= q_pos, _sortable(acc), jnp.int32(INT_MIN))
        key_ref[:, chunk(c)] = keys
        return carry
    lax.fori_loop(0, n_chunks, score_chunk, 0)

    def count(pred_fn):
        def body(c, acc):
            m = pred_fn(key_ref[:, chunk(c)], c)
            part = jnp.where(m, 1, 0)
            for g in range(tk // LANES):
                acc = acc + part[:, g * LANES:(g + 1) * LANES]
            return acc
        acc = lax.fori_loop(0, n_chunks, body, jnp.zeros((tq, LANES), I32))
        return jnp.sum(acc, axis=1, keepdims=True)

    biased = _search_bits(
        lambda cand: count(lambda k, c: k >= (cand ^ jnp.int32(INT_MIN))),
        tq, 32, lambda cnt: cnt >= topk)
    thr = biased ^ jnp.int32(INT_MIN)

    need = topk - count(lambda k, c: k > thr)
    jmax = _search_bits(
        lambda cand: count(lambda k, c: (k == thr) & (key_pos(c) < cand)),
        tq, pos_bits, lambda cnt: cnt < need)

    def bias_chunk(c, carry):
        k = key_ref[:, chunk(c)]
        kp = key_pos(c)
        sel = ((k > thr) | ((k == thr) & (kp <= jmax))) & (kp <= q_pos)
        bias_ref[:, chunk(c)] = jnp.where(sel, 0.0, NEG_BIG)
        return carry
    lax.fori_loop(0, n_chunks, bias_chunk, 0)

    scale = HEAD_DIM ** -0.5
    for h in range(N_HEADS):
        hs = slice(h * HEAD_DIM, (h + 1) * HEAD_DIM)
        qh = q_ref[:, hs]

        def attend(c, carry, hs=hs, qh=qh):
            m, l, acc = carry
            s = _dot_nt(qh, k_ref[chunk(c), hs]) * scale + bias_ref[:, chunk(c)]
            m_new = jnp.maximum(m, jnp.max(s, axis=1, keepdims=True))
            alpha = jnp.exp(m - m_new)
            p = jnp.exp(s - m_new)
            l = alpha * l + jnp.sum(p, axis=1, keepdims=True)
            acc = alpha * acc + _dot(p.astype(BF16), v_ref[chunk(c), hs])
            return m_new, l, acc

        init = (jnp.full((tq, 1), NEG_BIG, F32), jnp.zeros((tq, 1), F32), jnp.zeros((tq, HEAD_DIM), F32))
        _, l, acc = lax.fori_loop(0, n_chunks, attend, init)
        o_ref[:, hs] = (acc / l).astype(o_ref.dtype)


def dsa_prompt(qi, kiwi, kdup, q, k, v, *, batch, seq, tq, tk):
    nq = seq // tq
    topk = min(TOPK_MAX, seq // 4)
    row = lambda b, i: (b * nq + i, 0)
    whole = lambda b, i: (b, 0)
    return pl.pallas_call(
        functools.partial(_dsa_prompt_kernel, tq=tq, tk=tk, topk=topk, pos_bits=(seq - 1).bit_length()),
        grid=(batch, nq),
        in_specs=[
            pl.BlockSpec((tq, IDX_W), row),
            pl.BlockSpec((tq, LANES), row),
            pl.BlockSpec((seq, LANES), whole),
            pl.BlockSpec((tq, ATTN_W), row),
            pl.BlockSpec((seq, ATTN_W), whole),
            pl.BlockSpec((seq, ATTN_W), whole),
        ],
        out_specs=pl.BlockSpec((tq, ATTN_W), row),
        out_shape=jax.ShapeDtypeStruct((batch * seq, ATTN_W), BF16),
        scratch_shapes=[
            pltpu.VMEM((N_IDX_HEADS, tq, LANES), BF16),
            pltpu.VMEM((tq, seq), I32),
            pltpu.VMEM((tq, seq), F32),
        ],
        compiler_params=_params("parallel", "arbitrary"),
        name="dsa_prompt",
    )(qi, kiwi, kdup, q, k, v)


def _merge_kernel(p_ref, a_ref, wp_ref, wa_ref, gp_ref, ga_ref, o_ref):
    yp = _dot(p_ref[...], wp_ref[...])
    ya = _dot(a_ref[...], wa_ref[...])
    m = jax.nn.sigmoid(gp_ref[...]) * yp + jax.nn.sigmoid(ga_ref[...]) * ya
    o_ref[...] = m.astype(o_ref.dtype)


def merge(pool_y, attn_o, wp, wa, gl, layer, *, tm, tn):
    m, kp = pool_y.shape
    ka = attn_o.shape[1]
    nj = D_MODEL // tn
    return pl.pallas_call(
        _merge_kernel,
        grid=(m // tm, nj),
        in_specs=[
            pl.BlockSpec((tm, kp), lambda i, j: (i, 0)),
            pl.BlockSpec((tm, ka), lambda i, j: (i, 0)),
            pl.BlockSpec((None, kp, tn), lambda i, j: (layer, 0, j)),
            pl.BlockSpec((None, ka, tn), lambda i, j: (layer, 0, j)),
            pl.BlockSpec((tm, tn), lambda i, j: (i, j)),
            pl.BlockSpec((tm, tn), lambda i, j: (i, nj + j)),
        ],
        out_specs=pl.BlockSpec((tm, tn), lambda i, j: (i, j)),
        out_shape=jax.ShapeDtypeStruct((m, D_MODEL), BF16),
        compiler_params=_params("parallel", "arbitrary"),
        name="merge",
    )(pool_y, attn_o, wp, wa, gl, gl)


def _out_kernel(m_ref, w_ref, x_ref, o_ref):
    o_ref[...] = x_ref[...] + _dot(m_ref[...], w_ref[...])


def out_proj(mm, w, x, layer, *, tm, tn):
    m, k = mm.shape
    return pl.pallas_call(
        _out_kernel,
        grid=(m // tm, D_MODEL // tn),
        in_specs=[
            pl.BlockSpec((tm, k), lambda i, j: (i, 0)),
            pl.BlockSpec((None, k, tn), lambda i, j: (layer, 0, j)),
            pl.BlockSpec((tm, tn), lambda i, j: (i, j)),
        ],
        out_specs=pl.BlockSpec((tm, tn), lambda i, j: (i, j)),
        out_shape=jax.ShapeDtypeStruct((m, D_MODEL), F32),
        compiler_params=_params("parallel", "arbitrary"),
        name="out_proj",
    )(mm, w, x)


IDX_PG = 16
assert N_PAGES == LANES


def _sample_score_kernel(pt_ref, q_ref, w_ref, *refs):
    del pt_ref
    pages = refs[:IDX_PG]
    o_ref = refs[IDX_PG]
    q16 = q_ref[...]
    wcol = w_ref[:, 0:1]
    for p in range(IDX_PG):
        kp = pages[p][...].astype(BF16)
        s = jnp.maximum(_dot_nt(q16, kp), 0.0) * wcol
        o_ref[p:p + 1, :] = jnp.sum(s, axis=0, keepdims=True)


def sample_scores(page_table, qi3, wi3, cache_kidx, layer, *, n_seq):
    def page_spec(p):
        return pl.BlockSpec((None, None, PAGE_SIZE, IDX_HEAD_DIM),
                            lambda b, g, pt: (layer, pt[b, g * IDX_PG + p], 0, 0))
    grid_spec = pltpu.PrefetchScalarGridSpec(
        num_scalar_prefetch=1,
        grid=(n_seq, N_PAGES // IDX_PG),
        in_specs=[
            pl.BlockSpec((None, N_IDX_HEADS, IDX_HEAD_DIM), lambda b, g, pt: (b, 0, 0)),
            pl.BlockSpec((None, N_IDX_HEADS, 1), lambda b, g, pt: (b, 0, 0)),
        ] + [page_spec(p) for p in range(IDX_PG)],
        out_specs=pl.BlockSpec((None, IDX_PG, LANES), lambda b, g, pt: (b, g, 0)),
    )
    return pl.pallas_call(
        _sample_score_kernel,
        grid_spec=grid_spec,
        out_shape=jax.ShapeDtypeStruct((n_seq, N_PAGES, LANES), F32),
        compiler_params=_params("parallel", "arbitrary"),
        name="sample_scores",
    )(page_table, qi3, wi3, *([cache_kidx] * IDX_PG))


def _sample_select_kernel(sc_ref, q_ref, w_ref, kn_ref, idx_ref, rank_ref, *, topk):
    n_seq = sc_ref.shape[0]
    lane = lax.broadcasted_iota(I32, (N_PAGES, LANES), 1)
    page = lax.broadcasted_iota(I32, (N_PAGES, LANES), 0)
    pos = page * PAGE_SIZE + lane
    before = jnp.where(page < lane, 1.0, 0.0).astype(BF16)
    after = jnp.where(lane < page, 1.0, 0.0).astype(BF16)
    slot = lax.broadcasted_iota(I32, (topk, 1), 0)
    for b in range(n_seq):
        keys = _sortable(sc_ref[b])
        q16 = q_ref[b].astype(F32)
        kn = kn_ref[b:b + 1, 0:IDX_HEAD_DIM].astype(BF16).astype(F32)
        dh = jnp.sum(q16 * kn, axis=1, keepdims=True)
        s_new = jnp.sum(jnp.maximum(dh, 0.0) * w_ref[b], axis=0, keepdims=True)
        key_new = _sortable(s_new)

        def count(pred_cached, pred_new):
            c = jnp.sum(jnp.where(pred_cached, 1, 0), axis=1, keepdims=True)
            c = jnp.sum(c, axis=0, keepdims=True)
            return c + jnp.where(pred_new, 1, 0)

        def ge(cand, keys=keys, key_new=key_new):
            t = cand ^ jnp.int32(INT_MIN)
            return count(keys >= t, key_new >= t)

        biased = _search_bits(ge, 1, 32, lambda cnt: cnt >= topk)
        thr = biased ^ jnp.int32(INT_MIN)
        need = topk - count(keys > thr, key_new > thr)

        def eq_before(cand, keys=keys, key_new=key_new, thr=thr):
            return count((keys == thr) & (pos < cand), (key_new == thr) & (PAST_LEN < cand))

        jmax = _search_bits(eq_before, 1, PAST_LEN.bit_length(), lambda cnt, need=need: cnt < need)
        sel = (keys > thr) | ((keys == thr) & (pos <= jmax))
        sel_new = (key_new > thr) | ((key_new == thr) & (PAST_LEN <= jmax))

        sel_f = jnp.where(sel, 1.0, 0.0)
        in_row = _dot(sel_f.astype(BF16), before)
        tot = jnp.broadcast_to(jnp.sum(sel_f, axis=1, keepdims=True), (N_PAGES, LANES))
        rows_before = _dot(after, tot.astype(BF16))
        rank_ref[...] = jnp.where(sel, (in_row + rows_before).astype(I32), -1)

        def place(p, acc):
            hit = rank_ref[pl.ds(p, 1), :] == slot
            return acc + jnp.where(hit, p * PAGE_SIZE + lane[0:1, :], 0)

        acc = lax.fori_loop(0, N_PAGES, place, jnp.zeros((topk, LANES), I32))
        idx = jnp.sum(acc, axis=1, keepdims=True)
        idx_ref[b] = jnp.where(sel_new & (slot == topk - 1), PAST_LEN, idx)


def sample_select(scores, qi3, wi3, kiwi, *, n_seq):
    topk = min(TOPK_MAX, (PAST_LEN + 1) // 4)
    full = lambda a: pl.BlockSpec(a.shape, lambda i: (0,) * a.ndim)
    return pl.pallas_call(
        functools.partial(_sample_select_kernel, topk=topk),
        grid=(1,),
        in_specs=[full(scores), full(qi3), full(wi3), full(kiwi)],
        out_specs=pl.BlockSpec((n_seq, topk, 1), lambda i: (0, 0, 0)),
        out_shape=jax.ShapeDtypeStruct((n_seq, topk, 1), I32),
        scratch_shapes=[pltpu.VMEM((N_PAGES, LANES), I32)],
        compiler_params=_params("arbitrary"),
        name="sample_select",
    )(scores, qi3, wi3, kiwi)


def _sample_attn_kernel(pt_ref, idx_ref, q_ref, kn_ref, vn_ref, ck_ref, cv_ref, o_ref,
                        kbuf, vbuf, sem, *, layer, topk):
    b = pl.program_id(0)

    def copies(i):
        cached = jnp.minimum(idx_ref[b, i], PAST_LEN - 1)
        page = pt_ref[b, lax.shift_right_logical(cached, PAGE_SIZE.bit_length() - 1)]
        off = cached & (PAGE_SIZE - 1)
        return (pltpu.make_async_copy(ck_ref.at[layer, page, off], kbuf.at[i], sem.at[0]),
                pltpu.make_async_copy(cv_ref.at[layer, page, off], vbuf.at[i], sem.at[1]))

    def start(i, carry):
        for cp in copies(i):
            cp.start()
        return carry

    def wait(i, carry):
        for cp in copies(i):
            cp.wait()
        return carry

    lax.fori_loop(0, topk, start, 0)
    lax.fori_loop(0, topk, wait, 0)

    @pl.when(idx_ref[b, topk - 1] >= PAST_LEN)
    def _():
        kbuf[topk - 1] = kn_ref[...]
        vbuf[topk - 1] = vn_ref[...]

    rb = lambda a: a.astype(BF16).astype(F32)
    q = rb(q_ref[...])
    s = jnp.sum(rb(kbuf[...]) * q[None], axis=-1, keepdims=True) * (HEAD_DIM ** -0.5)
    m = jnp.max(s, axis=0, keepdims=True)
    p = jnp.exp(s - m)
    l = jnp.sum(p, axis=0)
    o_ref[...] = jnp.sum(rb(p) * rb(vbuf[...]), axis=0) / l


def sample_attn(page_table, idx, q3, kn3, vn3, cache_k, cache_v, layer, *, n_seq):
    topk = idx.shape[1]
    tile = pl.BlockSpec((None, N_HEADS, HEAD_DIM), lambda b, pt, ix: (b, 0, 0))
    grid_spec = pltpu.PrefetchScalarGridSpec(
        num_scalar_prefetch=2,
        grid=(n_seq,),
        in_specs=[tile, tile, tile, pl.BlockSpec(memory_space=pl.ANY), pl.BlockSpec(memory_space=pl.ANY)],
        out_specs=tile,
        scratch_shapes=[
            pltpu.VMEM((topk, N_HEADS, HEAD_DIM), F32),
            pltpu.VMEM((topk, N_HEADS, HEAD_DIM), F32),
            pltpu.SemaphoreType.DMA((2,)),
        ],
    )
    return pl.pallas_call(
        functools.partial(_sample_attn_kernel, layer=layer, topk=topk),
        grid_spec=grid_spec,
        out_shape=jax.ShapeDtypeStruct((n_seq, N_HEADS, HEAD_DIM), F32),
        compiler_params=_params("arbitrary"),
        name="sample_attn",
    )(page_table, idx, q3, kn3, vn3, cache_k, cache_v)


def _rot_tables(pos, head_dim, rot_dim, extra_scale_lanes=None):
    half = rot_dim // 2
    inv_freq = jnp.float32(ROPE_THETA) ** (-jnp.arange(half, dtype=F32) * (2.0 / rot_dim))
    ang = pos[:, None] * inv_freq[None, :]
    cos, sin = jnp.cos(ang), jnp.sin(ang)
    t = pos.shape[0]
    lane = jnp.arange(LANES) % head_dim
    cos_l = jnp.take(cos, lane % half, axis=1)
    sin_l = jnp.take(sin, lane % half, axis=1)
    first = (lane < half)[None, :]
    second = ((lane >= half) & (lane < rot_dim))[None, :]
    c = jnp.where(first | second, cos_l, 1.0)
    s_up = jnp.where(first, -sin_l, 0.0)
    s_dn = jnp.where(second, sin_l, 0.0)
    if extra_scale_lanes is not None:
        lo, hi, value = extra_scale_lanes
        full = jnp.arange(LANES)[None, :]
        c = jnp.where(full >= hi, 0.0, jnp.where(full >= lo, value, c))
        s_up = jnp.where(full >= lo, 0.0, s_up)
        s_dn = jnp.where(full >= lo, 0.0, s_dn)
    bc = lambda a: jnp.broadcast_to(a, (t, LANES)).astype(F32)
    return bc(c), bc(s_up), bc(s_dn)


def _tables(pos):
    wi_scale = float(N_IDX_HEADS * IDX_HEAD_DIM) ** -0.5
    return dict(
        qk=_rot_tables(pos, HEAD_DIM, ROT_DIM),
        idx=_rot_tables(pos, IDX_HEAD_DIM, IDX_ROT_DIM),
        kiwi=_rot_tables(pos, IDX_HEAD_DIM, IDX_ROT_DIM,
                         (IDX_HEAD_DIM, IDX_HEAD_DIM + N_IDX_HEADS, wi_scale)),
    )


def _project(xn, w_in, layer, tabs, *, seq, tm, tn, q_dtype):
    common = dict(tm=tm, tn=tn)
    rot = dict(seq=seq, **common)
    (u,) = proj(xn, w_in, layer, COL_U, POOL_W, [F32], **common)
    (q,) = proj(xn, w_in, layer, COL_Q, ATTN_W, [q_dtype], tables=tabs["qk"], rot_half=ROT_DIM // 2, **rot)
    k32, kbf = proj(xn, w_in, layer, COL_K, ATTN_W, [F32, BF16], tables=tabs["qk"], rot_half=ROT_DIM // 2, **rot)
    v32, vbf = proj(xn, w_in, layer, COL_V, ATTN_W, [F32, BF16], **common)
    (qi,) = proj(xn, w_in, layer, COL_QI, IDX_W, [BF16], tables=tabs["idx"], rot_half=IDX_ROT_DIM // 2, **rot)
    (gl,) = proj(xn, w_in, layer, COL_GL, 2 * D_MODEL, [F32], **common)
    kiwi, kdup = proj(xn, w_in, layer, COL_KIWI, LANES, [F32], tables=tabs["kiwi"],
                      rot_half=IDX_ROT_DIM // 2, dup=True, **rot)
    return u, q, k32, kbf, v32, vbf, qi, gl, kiwi, kdup


def kernel(x_prompt, x_sample, cache_k, cache_v, cache_kidx, state_pool, page_table,
           ffn1_norm, ffn1_w1, ffn1_w3, ffn1_w2, mix_norm, w_in, pool_group_w, pool_scale,
           w_up_pool, w_up_attn, w_out, ffn2_norm, ffn2_w1, ffn2_w3, ffn2_w2, final_norm):
    batch, seq, d = x_prompt.shape
    n_seq = x_sample.shape[0]
    depth = w_in.shape[0]

    bf = lambda a: a.astype(BF16)
    n_kiwi = IDX_HEAD_DIM + N_IDX_HEADS
    w_in_r = jnp.concatenate(
        [w_in[..., :COL_GL], w_in[..., COL_GL + n_kiwi:], w_in[..., COL_GL:COL_GL + n_kiwi],
         jnp.zeros((depth, d, LANES - n_kiwi), w_in.dtype)], axis=-1).astype(BF16)
    f1 = (bf(ffn1_w1), bf(ffn1_w3), bf(ffn1_w2))
    f2 = (bf(ffn2_w1), bf(ffn2_w3), bf(ffn2_w2))
    wg, wp, wa, wo = bf(pool_group_w), bf(w_up_pool), bf(w_up_attn), bf(w_out)
    row = lambda a, l: a[l][None, :]

    tabs_p = _tables(jnp.arange(seq, dtype=F32))
    tabs_s = _tables(jnp.full((SAMPLE_ROWS,), PAST_LEN, F32))

    pad_s = SAMPLE_ROWS - n_seq
    state_t = jnp.pad(jnp.swapaxes(state_pool, 1, 2), ((0, 0), (0, 0), (0, pad_s), (0, 0)))

    xp = x_prompt.reshape(batch * seq, d)
    xs = jnp.pad(x_sample.reshape(n_seq, d), ((0, pad_s), (0, 0)))
    fin = final_norm[None, :]
    outs = [[] for _ in range(8)]
    yp = ys = None
    for l in range(depth):
        xp, xn = ffn(xp, row(ffn1_norm, l), *f1, row(mix_norm, l), l, norm_dtype=BF16, tm=512, tf=512)
        u, q, k32, kbf, v32, vbf, qi, gl, kiwi, kdup = _project(
            xn, w_in_r, l, tabs_p, seq=seq, tm=1024, tn=512, q_dtype=BF16)
        pool_y = pool_prompt(u, wg, row(pool_scale, l), l, batch=batch, seq=seq, tm=512)
        attn_o = dsa_prompt(qi, kiwi, kdup, q, kbf, vbf, batch=batch, seq=seq, tq=256, tk=512)
        mm = merge(pool_y, attn_o, wp, wa, gl, l, tm=1024, tn=512)
        xp = out_proj(mm, wo, xp, l, tm=1024, tn=512)
        xp, yp = ffn(xp, row(ffn2_norm, l), *f2, fin, l, norm_dtype=F32, tm=512, tf=512)
        outs[0].append(k32.reshape(batch, seq, N_HEADS, HEAD_DIM))
        outs[1].append(v32.reshape(batch, seq, N_HEADS, HEAD_DIM))
        outs[2].append(kiwi[:, :IDX_HEAD_DIM].reshape(batch, seq, IDX_HEAD_DIM))
        outs[3].append(u.reshape(batch, seq, POOL_W)[:, seq - POOL_HIST:])

        xs, xn = ffn(xs, row(ffn1_norm, l), *f1, row(mix_norm, l), l, norm_dtype=BF16, tm=SAMPLE_ROWS, tf=512)
        u, q, k32, kbf, v32, vbf, qi, gl, kiwi, kdup = _project(
            xn, w_in_r, l, tabs_s, seq=SAMPLE_ROWS, tm=SAMPLE_ROWS, tn=512, q_dtype=F32)
        pool_y = pool_sample(u, state_t[l], wg, row(pool_scale, l), l)
        qi3 = qi.reshape(SAMPLE_ROWS, N_IDX_HEADS, IDX_HEAD_DIM)
        wi3 = kiwi[:, IDX_HEAD_DIM:n_kiwi].reshape(SAMPLE_ROWS, N_IDX_HEADS, 1)
        scores = sample_scores(page_table, qi3, wi3, cache_kidx, l, n_seq=n_seq)
        idx = sample_select(scores, qi3, wi3, kiwi, n_seq=n_seq).reshape(n_seq, -1)
        r3 = lambda a: a.reshape(SAMPLE_ROWS, N_HEADS, HEAD_DIM)
        a3 = sample_attn(page_table, idx, r3(q), r3(k32), r3(v32), cache_k, cache_v, l, n_seq=n_seq)
        attn_o = jnp.pad(a3.reshape(n_seq, ATTN_W), ((0, pad_s), (0, 0))).astype(BF16)
        mm = merge(pool_y, attn_o, wp, wa, gl, l, tm=SAMPLE_ROWS, tn=512)
        xs = out_proj(mm, wo, xs, l, tm=SAMPLE_ROWS, tn=512)
        xs, ys = ffn(xs, row(ffn2_norm, l), *f2, fin, l, norm_dtype=F32, tm=SAMPLE_ROWS, tf=512)
        outs[4].append(k32[:n_seq].reshape(n_seq, 1, N_HEADS, HEAD_DIM))
        outs[5].append(v32[:n_seq].reshape(n_seq, 1, N_HEADS, HEAD_DIM))
        outs[6].append(kiwi[:n_seq, :IDX_HEAD_DIM].reshape(n_seq, 1, IDX_HEAD_DIM))
        outs[7].append(jnp.concatenate([state_pool[l][:, 1:], u[:n_seq, None, :]], axis=1))

    y_prompt = yp.reshape(batch, seq, d)
    y_sample = ys[:n_seq].reshape(n_seq, 1, d)
    st = [jnp.stack(o) for o in outs]
    return (y_prompt, y_sample, st[0], st[1], st[2], st[3], st[4], st[5], st[6], st[7])
```

```python
import functools

import jax
import jax.numpy as jnp
from jax import lax
from jax.experimental import pallas as pl
from jax.experimental.pallas import tpu as pltpu

F32 = jnp.float32
BF16 = jnp.bfloat16
I32 = jnp.int32

D_MODEL = 2048
DEPTH = 4
PAST_LEN = 16384
PAGE_SIZE = 128
N_PAGES = PAST_LEN // PAGE_SIZE
N_HEADS = 8
HEAD_DIM = 128
ATTN_W = N_HEADS * HEAD_DIM
ROT_DIM = HEAD_DIM // 4
ROPE_THETA = 500000.0
N_IDX_HEADS = 16
IDX_HEAD_DIM = 64
IDX_W = N_IDX_HEADS * IDX_HEAD_DIM
IDX_ROT_DIM = IDX_HEAD_DIM // 4
TOPK_MAX = 256
POOL_WINDOWS = (2, 4, 8, 16)
POOL_GROUP_W = 256
POOL_W = len(POOL_WINDOWS) * POOL_GROUP_W
POOL_HIST = max(POOL_WINDOWS) - 1
D_FF = 5632
NORM_EPS = 1e-6

LANES = 128
SUBLANES = 8
HALO = 16
VMEM_LIMIT = 56 * 1024 * 1024
INT_MIN = -2 ** 31
NEG_BIG = -1e30
SAMPLE_ROWS = 16

N_MAIN = POOL_W + 3 * ATTN_W + IDX_W
N_KIWI = IDX_HEAD_DIM + N_IDX_HEADS


def _params(*sem):
    return pltpu.CompilerParams(dimension_semantics=sem, vmem_limit_bytes=VMEM_LIMIT)


def _rms(x, g):
    y = x * lax.rsqrt(jnp.mean(x * x, axis=-1, keepdims=True) + NORM_EPS)
    return y * g


def _dot(a, b):
    return jnp.dot(a, b, preferred_element_type=F32)


def _dot_nt(a, b):
    return lax.dot_general(a, b, (((1,), (1,)), ((), ())), preferred_element_type=F32)


def _ffn_kernel(x_ref, g_ref, w1_ref, w3_ref, w2_ref, g2_ref, o_ref, n_ref, xn_ref, acc_ref):
    f = pl.program_id(1)

    @pl.when(f == 0)
    def _():
        xn_ref[...] = _rms(x_ref[...], g_ref[...]).astype(BF16)
        acc_ref[...] = jnp.zeros_like(acc_ref)

    xn = xn_ref[...]
    a = _dot(xn, w1_ref[...])
    b = _dot(xn, w3_ref[...])
    h = (a * jax.nn.sigmoid(a)) * b
    acc_ref[...] += _dot(h.astype(BF16), w2_ref[...])

    @pl.when(f == pl.num_programs(1) - 1)
    def _():
        out = x_ref[...] + 0.5 * acc_ref[...]
        o_ref[...] = out
        n_ref[...] = _rms(out, g2_ref[...]).astype(n_ref.dtype)


def ffn(x, g, w1, w3, w2, g2, layer, *, norm_dtype, tm, tf):
    m, d = x.shape
    d_ff = w1.shape[2]
    return pl.pallas_call(
        _ffn_kernel,
        grid=(m // tm, d_ff // tf),
        in_specs=[
            pl.BlockSpec((tm, d), lambda i, f: (i, 0)),
            pl.BlockSpec((1, d), lambda i, f: (0, 0)),
            pl.BlockSpec((None, d, tf), lambda i, f: (layer, 0, f)),
            pl.BlockSpec((None, d, tf), lambda i, f: (layer, 0, f)),
            pl.BlockSpec((None, tf, d), lambda i, f: (layer, f, 0)),
            pl.BlockSpec((1, d), lambda i, f: (0, 0)),
        ],
        out_specs=[
            pl.BlockSpec((tm, d), lambda i, f: (i, 0)),
            pl.BlockSpec((tm, d), lambda i, f: (i, 0)),
        ],
        out_shape=[jax.ShapeDtypeStruct((m, d), F32), jax.ShapeDtypeStruct((m, d), norm_dtype)],
        scratch_shapes=[pltpu.VMEM((tm, d), BF16), pltpu.VMEM((tm, d), F32)],
        compiler_params=_params("parallel", "arbitrary"),
        name="ffn",
    )(x, g, w1, w3, w2, g2)


def _rotate(z, c, s_up, s_dn, half):
    outs = []
    for g in range(z.shape[1] // LANES):
        zg = z[:, g * LANES:(g + 1) * LANES]
        up = pltpu.roll(zg, LANES - half, 1)
        dn = pltpu.roll(zg, half, 1)
        outs.append(zg * c + up * s_up + dn * s_dn)
    return outs[0] if len(outs) == 1 else jnp.concatenate(outs, axis=1)


PROJ_TN = 512
_STEPS = dict(u=(0, 2), q=(2, 4), k=(4, 6), v=(6, 8), qi=(8, 10), gl=(10, 18), kiwi=(18, 19))
N_PROJ_STEPS = 19
assert POOL_W == ATTN_W == IDX_W == 2 * PROJ_TN and 2 * D_MODEL == 8 * PROJ_TN


def _proj_kernel(a_ref, wm_ref, wg_ref, wk_ref, cq, uq, dq, ci, ui, di, ck, uk, dk,
                 u_ref, q_ref, k32_ref, kbf_ref, v32_ref, qi_ref, gl_ref, kiwi_ref, kdup_ref, vt_ref=None):
    j = pl.program_id(1)
    a = a_ref[...]

    def during(name):
        lo, hi = _STEPS[name]
        return pl.when((j >= lo) & (j < hi))

    @during("u")
    def _():
        u_ref[...] = _dot(a, wm_ref[...])

    @during("q")
    def _():
        z = _rotate(_dot(a, wm_ref[...]), cq[...], uq[...], dq[...], ROT_DIM // 2)
        q_ref[...] = z.astype(q_ref.dtype)

    @during("k")
    def _():
        z = _rotate(_dot(a, wm_ref[...]), cq[...], uq[...], dq[...], ROT_DIM // 2)
        k32_ref[...] = z
        kbf_ref[...] = z.astype(BF16)

    @during("v")
    def _():
        z = _dot(a, wm_ref[...])
        v32_ref[...] = z
        if vt_ref is not None:
            vt_ref[...] = jnp.transpose(z).astype(BF16)

    @during("qi")
    def _():
        z = _rotate(_dot(a, wm_ref[...]), ci[...], ui[...], di[...], IDX_ROT_DIM // 2)
        qi_ref[...] = z.astype(BF16)

    @during("gl")
    def _():
        gl_ref[...] = _dot(a, wg_ref[...])

    @during("kiwi")
    def _():
        z = _rotate(_dot(a, wk_ref[...]), ck[...], uk[...], dk[...], IDX_ROT_DIM // 2)
        kiwi_ref[...] = z
        lane = lax.broadcasted_iota(I32, z.shape, 1)
        kdup_ref[...] = jnp.where(lane < IDX_HEAD_DIM, z, pltpu.roll(z, IDX_HEAD_DIM, 1)).astype(BF16)


def proj(a, wm, wg, wk, layer, tabs, *, seq, tm, q_dtype, emit_vt):
    m, k = a.shape
    nt = seq // tm
    tn = PROJ_TN
    v_lo = _STEPS["v"][0]
    vt_spec = pl.BlockSpec((None, tn, tm), lambda i, j: (i // nt, jnp.clip(j - v_lo, 0, 1), i % nt))
    vt_shape = jax.ShapeDtypeStruct((m // seq, ATTN_W, seq), BF16)

    def col(name, n_blocks):
        lo = _STEPS[name][0]
        return lambda i, j: (i, jnp.clip(j - lo, 0, n_blocks - 1))

    tspec = pl.BlockSpec((tm, LANES), lambda i, j: (i % nt, 0))
    wide = lambda name: pl.BlockSpec((tm, tn), col(name, 2))
    narrow = pl.BlockSpec((tm, LANES), lambda i, j: (i, 0))
    sds = lambda n, dt: jax.ShapeDtypeStruct((m, n), dt)
    return pl.pallas_call(
        _proj_kernel,
        grid=(m // tm, N_PROJ_STEPS),
        in_specs=[
            pl.BlockSpec((tm, k), lambda i, j: (i, 0)),
            pl.BlockSpec((None, k, tn), lambda i, j: (layer, 0, jnp.minimum(j, 9))),
            pl.BlockSpec((None, k, tn), lambda i, j: (layer, 0, jnp.clip(j - 10, 0, 7))),
            pl.BlockSpec((None, k, LANES), lambda i, j: (layer, 0, 0)),
        ] + [tspec] * 9,
        out_specs=[wide("u"), wide("q"), wide("k"), wide("k"), wide("v"), wide("qi"),
                   pl.BlockSpec((tm, tn), col("gl", 8)), narrow, narrow] + [vt_spec] * emit_vt,
        out_shape=[sds(POOL_W, F32), sds(ATTN_W, q_dtype), sds(ATTN_W, F32), sds(ATTN_W, BF16),
                   sds(ATTN_W, F32), sds(IDX_W, BF16), sds(2 * D_MODEL, F32),
                   sds(LANES, F32), sds(LANES, BF16)] + [vt_shape] * emit_vt,
        compiler_params=_params("parallel", "arbitrary"),
        name="proj",
    )(a, wm, wg, wk, *tabs["qk"], *tabs["idx"], *tabs["kiwi"])


def _pool_groups(buf_ref, u, cnt_fn, wg_ref, sc_ref, o_ref, tm):
    for g, w in enumerate(POOL_WINDOWS):
        lo, hi = g * POOL_GROUP_W, (g + 1) * POOL_GROUP_W
        s = buf_ref[HALO:HALO + tm, lo:hi]
        for j in range(1, w):
            s = s + buf_ref[HALO - j:HALO - j + tm, lo:hi]
        d = s / cnt_fn(w) - u[:, lo:hi]
        y = _dot(d.astype(BF16), wg_ref[g]) * sc_ref[:, lo:hi]
        o_ref[:, lo:hi] = y.astype(o_ref.dtype)


def _pool_prompt_kernel(u_ref, prev_ref, wg_ref, sc_ref, o_ref, buf_ref, *, tm):
    i = pl.program_id(1)
    u = u_ref[...]
    buf_ref[0:HALO, :] = jnp.where(i == 0, 0.0, prev_ref[...])
    buf_ref[HALO:HALO + tm, :] = u
    pos = (i * tm + lax.broadcasted_iota(I32, (tm, 1), 0)).astype(F32)
    _pool_groups(buf_ref, u, lambda w: jnp.minimum(pos + 1.0, float(w)), wg_ref, sc_ref, o_ref, tm)


def pool_prompt(u, wg, sc, layer, *, batch, seq, tm):
    nt = seq // tm
    hb = tm // HALO
    return pl.pallas_call(
        functools.partial(_pool_prompt_kernel, tm=tm),
        grid=(batch, nt),
        in_specs=[
            pl.BlockSpec((tm, POOL_W), lambda b, i: (b * nt + i, 0)),
            pl.BlockSpec((HALO, POOL_W), lambda b, i: (jnp.maximum((b * nt + i) * hb - 1, 0), 0)),
            pl.BlockSpec((None, len(POOL_WINDOWS), POOL_GROUP_W, POOL_GROUP_W), lambda b, i: (layer, 0, 0, 0)),
            pl.BlockSpec((1, POOL_W), lambda b, i: (0, 0)),
        ],
        out_specs=pl.BlockSpec((tm, POOL_W), lambda b, i: (b * nt + i, 0)),
        out_shape=jax.ShapeDtypeStruct((batch * seq, POOL_W), BF16),
        scratch_shapes=[pltpu.VMEM((HALO + tm, POOL_W), F32)],
        compiler_params=_params("parallel", "arbitrary"),
        name="pool_prompt",
    )(u, u, wg, sc)


def _pool_sample_kernel(u_ref, st_ref, wg_ref, sc_ref, o_ref):
    u = u_ref[...]
    for g, w in enumerate(POOL_WINDOWS):
        lo, hi = g * POOL_GROUP_W, (g + 1) * POOL_GROUP_W
        s = u[:, lo:hi]
        for j in range(1, w):
            s = s + st_ref[POOL_HIST - j, :, lo:hi]
        d = s / float(w) - u[:, lo:hi]
        y = _dot(d.astype(BF16), wg_ref[g]) * sc_ref[:, lo:hi]
        o_ref[:, lo:hi] = y.astype(o_ref.dtype)


def pool_sample(u, state_t, wg, sc, layer):
    rows = u.shape[0]
    return pl.pallas_call(
        _pool_sample_kernel,
        grid=(1,),
        in_specs=[
            pl.BlockSpec((rows, POOL_W), lambda i: (0, 0)),
            pl.BlockSpec((POOL_HIST, rows, POOL_W), lambda i: (0, 0, 0)),
            pl.BlockSpec((None, len(POOL_WINDOWS), POOL_GROUP_W, POOL_GROUP_W), lambda i: (layer, 0, 0, 0)),
            pl.BlockSpec((1, POOL_W), lambda i: (0, 0)),
        ],
        out_specs=pl.BlockSpec((rows, POOL_W), lambda i: (0, 0)),
        out_shape=jax.ShapeDtypeStruct((rows, POOL_W), BF16),
        compiler_params=_params("arbitrary"),
        name="pool_sample",
    )(u, state_t, wg, sc)


def _sortable(score):
    bits = lax.bitcast_convert_type(score, I32)
    return jnp.where(bits >= 0, bits, bits ^ jnp.int32(0x7FFFFFFF))


def _search_bits(count_fn, shape, nbits, accept, count0):
    def step(i, carry):
        acc, cnt_acc = carry
        bit = jnp.left_shift(jnp.int32(1), (nbits - 1) - i)
        cand = acc | bit
        cnt = count_fn(cand)
        ok = accept(cnt)
        return jnp.where(ok, cand, acc), jnp.where(ok, cnt, cnt_acc)
    init = (jnp.zeros(shape, I32), jnp.broadcast_to(count0, shape).astype(I32))
    return lax.fori_loop(0, nbits, step, init)


COUNT_SUMS = 8


def _dsa_prompt_kernel(qi_ref, kw_ref, kd_ref, q_ref, k_ref, vt_ref, o_ref,
                       qm_ref, key_ref, bias_ref, acc_ref, m_ref, l_ref, *, tq, tk, ta, topk, pos_bits):
    qb = pl.program_id(1)
    n_chunks = (qb * tq + tq + tk - 1) // tk
    q_pos = qb * tq + lax.broadcasted_iota(I32, (1, tq), 1)
    lane = lax.broadcasted_iota(I32, (tq, LANES), 1)

    for hp in range(N_IDX_HEADS // 2):
        pair = qi_ref[:, hp * LANES:(hp + 1) * LANES]
        qm_ref[2 * hp] = jnp.where(lane < IDX_HEAD_DIM, pair, jnp.zeros_like(pair))
        qm_ref[2 * hp + 1] = jnp.where(lane >= IDX_HEAD_DIM, pair, jnp.zeros_like(pair))
    wi_t = jnp.transpose(kw_ref[...])

    def key_pos(c):
        return c * tk + lax.broadcasted_iota(I32, (tk, 1), 0)

    def chunk(c):
        return pl.ds(pl.multiple_of(c * tk, tk), tk)

    def score_chunk(c, carry):
        kc = kd_ref[chunk(c), :]
        acc = jnp.zeros((tk, tq), F32)
        for h in range(N_IDX_HEADS):
            s = _dot_nt(kc, qm_ref[h])
            acc = acc + jnp.maximum(s, 0.0) * wi_t[IDX_HEAD_DIM + h:IDX_HEAD_DIM + h + 1, :]
        keys = jnp.where(key_pos(c) <= q_pos, _sortable(acc), jnp.int32(INT_MIN))
        key_ref[chunk(c), :] = keys
        return carry
    lax.fori_loop(0, n_chunks, score_chunk, 0)

    def count(pred_fn):
        def body(c, accs):
            part = jnp.where(pred_fn(key_ref[chunk(c), :], c), 1, 0)
            accs = list(accs)
            for r in range(tk // SUBLANES):
                a = r % len(accs)
                accs[a] = accs[a] + part[r * SUBLANES:(r + 1) * SUBLANES, :]
            return tuple(accs)
        zero = jnp.zeros((SUBLANES, tq), I32)
        accs = lax.fori_loop(0, n_chunks, body, (zero,) * COUNT_SUMS)
        return jnp.sum(sum(accs[1:], accs[0]), axis=0, keepdims=True)

    biased, n_ge = _search_bits(
        lambda cand: count(lambda k, c: k >= (cand ^ jnp.int32(INT_MIN))),
        (1, tq), 32, lambda cnt: cnt >= topk, n_chunks * tk)
    thr = biased ^ jnp.int32(INT_MIN)

    tied = (n_ge > topk) & (thr != jnp.int32(INT_MIN))

    def break_ties():
        need = topk - count(lambda k, c: k > thr)
        jm, _ = _search_bits(
            lambda cand: count(lambda k, c: (k == thr) & (key_pos(c) < cand)),
            (1, tq), pos_bits, lambda cnt: cnt < need, 0)
        return jm

    jmax = lax.cond(jnp.max(jnp.where(tied, 1, 0)) > 0, break_ties,
                    lambda: jnp.full((1, tq), 2 ** 31 - 1, I32))

    def bias_chunk(c, carry):
        k = key_ref[chunk(c), :]
        kp = key_pos(c)
        sel = ((k > thr) | ((k == thr) & (kp <= jmax))) & (kp <= q_pos)
        bias_ref[chunk(c), :] = jnp.where(sel, 0.0, NEG_BIG)
        return carry
    lax.fori_loop(0, n_chunks, bias_chunk, 0)

    scale = HEAD_DIM ** -0.5
    acc_ref[...] = jnp.zeros_like(acc_ref)
    m_ref[...] = jnp.full(m_ref.shape, NEG_BIG, F32)
    l_ref[...] = jnp.zeros_like(l_ref)

    def attend(c, carry):
        rows = pl.ds(pl.multiple_of(c * ta, ta), ta)
        heads = [slice(h * HEAD_DIM, (h + 1) * HEAD_DIM) for h in range(N_HEADS)]
        raw = [_dot_nt(k_ref[rows, hs], q_ref[:, hs]) for hs in heads]
        bias = bias_ref[rows, :]
        probs, alphas = [], []
        for h in range(N_HEADS):
            s = raw[h] * scale + bias
            m_old = m_ref[h, 0:1, :]
            m_new = jnp.maximum(m_old, jnp.max(s, axis=0, keepdims=True))
            alpha = jnp.exp(m_old - m_new)
            p = jnp.exp(s - m_new)
            l_new = alpha * l_ref[h, 0:1, :] + jnp.sum(p, axis=0, keepdims=True)
            m_ref[h] = jnp.broadcast_to(m_new, (SUBLANES, tq))
            l_ref[h] = jnp.broadcast_to(l_new, (SUBLANES, tq))
            probs.append(p.astype(BF16))
            alphas.append(alpha)
        for h, hs in enumerate(heads):
            acc_ref[hs, :] = alphas[h] * acc_ref[hs, :] + _dot(vt_ref[hs, rows], probs[h])
        return carry
    lax.fori_loop(0, (qb * tq + tq + ta - 1) // ta, attend, 0)
    for h in range(N_HEADS):
        hs = slice(h * HEAD_DIM, (h + 1) * HEAD_DIM)
        o_ref[:, hs] = jnp.transpose(acc_ref[hs, :] / l_ref[h, 0:1, :]).astype(o_ref.dtype)


def dsa_prompt(qi, kiwi, kdup, q, k, vt, *, batch, seq, tq, tk, ta):
    nq = seq // tq
    topk = min(TOPK_MAX, seq // 4)
    row = lambda b, i: (b * nq + i, 0)
    whole = lambda b, i: (b, 0)
    return pl.pallas_call(
        functools.partial(_dsa_prompt_kernel, tq=tq, tk=tk, ta=ta, topk=topk, pos_bits=(seq - 1).bit_length()),
        grid=(batch, nq),
        in_specs=[
            pl.BlockSpec((tq, IDX_W), row),
            pl.BlockSpec((tq, LANES), row),
            pl.BlockSpec((seq, LANES), whole),
            pl.BlockSpec((tq, ATTN_W), row),
            pl.BlockSpec((seq, ATTN_W), whole),
            pl.BlockSpec((None, ATTN_W, seq), lambda b, i: (b, 0, 0)),
        ],
        out_specs=pl.BlockSpec((tq, ATTN_W), row),
        out_shape=jax.ShapeDtypeStruct((batch * seq, ATTN_W), BF16),
        scratch_shapes=[
            pltpu.VMEM((N_IDX_HEADS, tq, LANES), BF16),
            pltpu.VMEM((seq, tq), I32),
            pltpu.VMEM((seq, tq), F32),
            pltpu.VMEM((ATTN_W, tq), F32),
            pltpu.VMEM((N_HEADS, SUBLANES, tq), F32),
            pltpu.VMEM((N_HEADS, SUBLANES, tq), F32),
        ],
        compiler_params=_params("parallel", "arbitrary"),
        name="dsa_prompt",
    )(qi, kiwi, kdup, q, k, vt)


def _merge_kernel(p_ref, a_ref, wp_ref, wa_ref, gp_ref, ga_ref, o_ref):
    yp = _dot(p_ref[...], wp_ref[...])
    ya = _dot(a_ref[...], wa_ref[...])
    m = jax.nn.sigmoid(gp_ref[...]) * yp + jax.nn.sigmoid(ga_ref[...]) * ya
    o_ref[...] = m.astype(o_ref.dtype)


def merge(pool_y, attn_o, wp, wa, gl, layer, *, tm, tn):
    m, kp = pool_y.shape
    ka = attn_o.shape[1]
    nj = D_MODEL // tn
    return pl.pallas_call(
        _merge_kernel,
        grid=(m // tm, nj),
        in_specs=[
            pl.BlockSpec((tm, kp), lambda i, j: (i, 0)),
            pl.BlockSpec((tm, ka), lambda i, j: (i, 0)),
            pl.BlockSpec((None, kp, tn), lambda i, j: (layer, 0, j)),
            pl.BlockSpec((None, ka, tn), lambda i, j: (layer, 0, j)),
            pl.BlockSpec((tm, tn), lambda i, j: (i, j)),
            pl.BlockSpec((tm, tn), lambda i, j: (i, nj + j)),
        ],
        out_specs=pl.BlockSpec((tm, tn), lambda i, j: (i, j)),
        out_shape=jax.ShapeDtypeStruct((m, D_MODEL), BF16),
        compiler_params=_params("parallel", "arbitrary"),
        name="merge",
    )(pool_y, attn_o, wp, wa, gl, gl)


def _out_kernel(m_ref, w_ref, x_ref, o_ref):
    o_ref[...] = x_ref[...] + _dot(m_ref[...], w_ref[...])


def out_proj(mm, w, x, layer, *, tm, tn):
    m, k = mm.shape
    return pl.pallas_call(
        _out_kernel,
        grid=(m // tm, D_MODEL // tn),
        in_specs=[
            pl.BlockSpec((tm, k), lambda i, j: (i, 0)),
            pl.BlockSpec((None, k, tn), lambda i, j: (layer, 0, j)),
            pl.BlockSpec((tm, tn), lambda i, j: (i, j)),
        ],
        out_specs=pl.BlockSpec((tm, tn), lambda i, j: (i, j)),
        out_shape=jax.ShapeDtypeStruct((m, D_MODEL), F32),
        compiler_params=_params("parallel", "arbitrary"),
        name="out_proj",
    )(mm, w, x)


IDX_PG = 16
assert N_PAGES == LANES


def _sample_score_kernel(pt_ref, q_ref, w_ref, *refs):
    del pt_ref
    pages = refs[:IDX_PG]
    o_ref = refs[IDX_PG]
    q16 = q_ref[...]
    wcol = w_ref[:, 0:1]
    for p in range(IDX_PG):
        kp = pages[p][...].astype(BF16)
        s = jnp.maximum(_dot(q16, kp), 0.0) * wcol
        o_ref[p:p + 1, :] = jnp.sum(s, axis=0, keepdims=True)


def sample_scores(page_table, qi3, wi3, cache_kidx_t, layer, *, n_seq):
    def page_spec(p):
        return pl.BlockSpec((None, None, IDX_HEAD_DIM, PAGE_SIZE),
                            lambda b, g, pt: (layer, pt[b, g * IDX_PG + p], 0, 0))
    grid_spec = pltpu.PrefetchScalarGridSpec(
        num_scalar_prefetch=1,
        grid=(n_seq, N_PAGES // IDX_PG),
        in_specs=[
            pl.BlockSpec((None, N_IDX_HEADS, IDX_HEAD_DIM), lambda b, g, pt: (b, 0, 0)),
            pl.BlockSpec((None, N_IDX_HEADS, 1), lambda b, g, pt: (b, 0, 0)),
        ] + [page_spec(p) for p in range(IDX_PG)],
        out_specs=pl.BlockSpec((None, IDX_PG, LANES), lambda b, g, pt: (b, g, 0)),
    )
    return pl.pallas_call(
        _sample_score_kernel,
        grid_spec=grid_spec,
        out_shape=jax.ShapeDtypeStruct((n_seq, N_PAGES, LANES), F32),
        compiler_params=_params("parallel", "arbitrary"),
        name="sample_scores",
    )(page_table, qi3, wi3, *([cache_kidx_t] * IDX_PG))


def _sample_select_kernel(sc_ref, q_ref, w_ref, kn_ref, idx_ref, rank_ref, *, topk):
    n_seq = sc_ref.shape[0]
    lane = lax.broadcasted_iota(I32, (N_PAGES, LANES), 1)
    page = lax.broadcasted_iota(I32, (N_PAGES, LANES), 0)
    pos = page * PAGE_SIZE + lane
    before = jnp.where(page < lane, 1.0, 0.0).astype(BF16)
    after = jnp.where(lane < page, 1.0, 0.0).astype(BF16)
    slot = lax.broadcasted_iota(I32, (topk, 1), 0)

    keys = _sortable(sc_ref[...])
    kn = kn_ref[0:n_seq, :, 0:IDX_HEAD_DIM].astype(BF16).astype(F32)
    dh = jnp.sum(q_ref[0:n_seq].astype(F32) * kn, axis=2, keepdims=True)
    s_new = jnp.sum(jnp.maximum(dh, 0.0) * w_ref[0:n_seq], axis=1, keepdims=True)
    key_new = _sortable(s_new)
    state = (n_seq, 1, 1)

    def count(pred_cached, pred_new):
        c = jnp.sum(jnp.where(pred_cached, 1, 0), axis=2, keepdims=True)
        c = jnp.sum(c, axis=1, keepdims=True)
        return c + jnp.where(pred_new, 1, 0)

    def ge(cand):
        t = cand ^ jnp.int32(INT_MIN)
        return count(keys >= t, key_new >= t)

    biased, _ = _search_bits(ge, state, 32, lambda cnt: cnt >= topk, PAST_LEN + 1)
    thr = biased ^ jnp.int32(INT_MIN)
    need = topk - count(keys > thr, key_new > thr)

    def eq_before(cand):
        return count((keys == thr) & (pos[None] < cand), (key_new == thr) & (PAST_LEN < cand))

    jmax, _ = _search_bits(eq_before, state, PAST_LEN.bit_length(), lambda cnt: cnt < need, 0)
    sel_all = (keys > thr) | ((keys == thr) & (pos[None] <= jmax))
    sel_new_all = (key_new > thr) | ((key_new == thr) & (PAST_LEN <= jmax))

    for b in range(n_seq):
        sel = sel_all[b]
        sel_new = sel_new_all[b]
        sel_f = jnp.where(sel, 1.0, 0.0)
        in_row = _dot(sel_f.astype(BF16), before)
        tot = jnp.broadcast_to(jnp.sum(sel_f, axis=1, keepdims=True), (N_PAGES, LANES))
        rows_before = _dot(after, tot.astype(BF16))
        rank_ref[...] = jnp.where(sel, (in_row + rows_before).astype(I32), -1)

        def place(p, acc):
            hit = rank_ref[pl.ds(p, 1), :] == slot
            return acc + jnp.where(hit, p * PAGE_SIZE + lane[0:1, :], 0)

        acc = lax.fori_loop(0, N_PAGES, place, jnp.zeros((topk, LANES), I32))
        idx = jnp.sum(acc, axis=1, keepdims=True)
        idx_ref[b] = jnp.where(sel_new & (slot == topk - 1), PAST_LEN, idx)


def sample_select(scores, qi3, wi3, kiwi, *, n_seq):
    topk = min(TOPK_MAX, (PAST_LEN + 1) // 4)
    full = lambda a: pl.BlockSpec(a.shape, lambda i: (0,) * a.ndim)
    return pl.pallas_call(
        functools.partial(_sample_select_kernel, topk=topk),
        grid=(1,),
        in_specs=[full(scores), full(qi3), full(wi3), full(kiwi)],
        out_specs=pl.BlockSpec((n_seq, topk, 1), lambda i: (0, 0, 0)),
        out_shape=jax.ShapeDtypeStruct((n_seq, topk, 1), I32),
        scratch_shapes=[pltpu.VMEM((N_PAGES, LANES), I32)],
        compiler_params=_params("arbitrary"),
        name="sample_select",
    )(scores, qi3, wi3, kiwi)


def _sample_attn_kernel(pt_ref, idx_ref, q_ref, kn_ref, vn_ref, ck_ref, cv_ref, o_ref,
                        kbuf, vbuf, sem, *, layer, topk):
    b = pl.program_id(0)

    def copies(i):
        cached = jnp.minimum(idx_ref[b, i], PAST_LEN - 1)
        page = pt_ref[b, lax.shift_right_logical(cached, PAGE_SIZE.bit_length() - 1)]
        off = cached & (PAGE_SIZE - 1)
        return (pltpu.make_async_copy(ck_ref.at[layer, page, off], kbuf.at[i], sem.at[0]),
                pltpu.make_async_copy(cv_ref.at[layer, page, off], vbuf.at[i], sem.at[1]))

    def start(i, carry):
        for cp in copies(i):
            cp.start()
        return carry

    def wait(i, carry):
        for cp in copies(i):
            cp.wait()
        return carry

    lax.fori_loop(0, topk, start, 0)
    lax.fori_loop(0, topk, wait, 0)

    @pl.when(idx_ref[b, topk - 1] >= PAST_LEN)
    def _():
        kbuf[topk - 1] = kn_ref[...]
        vbuf[topk - 1] = vn_ref[...]

    rb = lambda a: a.astype(BF16).astype(F32)
    q = rb(q_ref[...])
    s = jnp.sum(rb(kbuf[...]) * q[None], axis=-1, keepdims=True) * (HEAD_DIM ** -0.5)
    m = jnp.max(s, axis=0, keepdims=True)
    p = jnp.exp(s - m)
    l = jnp.sum(p, axis=0)
    o_ref[...] = jnp.sum(rb(p) * rb(vbuf[...]), axis=0) / l


def sample_attn(page_table, idx, q3, kn3, vn3, cache_k, cache_v, layer, *, n_seq):
    topk = idx.shape[1]
    tile = pl.BlockSpec((None, N_HEADS, HEAD_DIM), lambda b, pt, ix: (b, 0, 0))
    grid_spec = pltpu.PrefetchScalarGridSpec(
        num_scalar_prefetch=2,
        grid=(n_seq,),
        in_specs=[tile, tile, tile, pl.BlockSpec(memory_space=pl.ANY), pl.BlockSpec(memory_space=pl.ANY)],
        out_specs=tile,
        scratch_shapes=[
            pltpu.VMEM((topk, N_HEADS, HEAD_DIM), F32),
            pltpu.VMEM((topk, N_HEADS, HEAD_DIM), F32),
            pltpu.SemaphoreType.DMA((2,)),
        ],
    )
    return pl.pallas_call(
        functools.partial(_sample_attn_kernel, layer=layer, topk=topk),
        grid_spec=grid_spec,
        out_shape=jax.ShapeDtypeStruct((n_seq, N_HEADS, HEAD_DIM), F32),
        compiler_params=_params("arbitrary"),
        name="sample_attn",
    )(page_table, idx, q3, kn3, vn3, cache_k, cache_v)


def _rot_tables(pos, head_dim, rot_dim, extra_scale_lanes=None):
    half = rot_dim // 2
    inv_freq = jnp.float32(ROPE_THETA) ** (-jnp.arange(half, dtype=F32) * (2.0 / rot_dim))
    ang = pos[:, None] * inv_freq[None, :]
    cos, sin = jnp.cos(ang), jnp.sin(ang)
    t = pos.shape[0]
    lane = jnp.arange(LANES) % head_dim
    cos_l = jnp.take(cos, lane % half, axis=1)
    sin_l = jnp.take(sin, lane % half, axis=1)
    first = (lane < half)[None, :]
    second = ((lane >= half) & (lane < rot_dim))[None, :]
    c = jnp.where(first | second, cos_l, 1.0)
    s_up = jnp.where(first, -sin_l, 0.0)
    s_dn = jnp.where(second, sin_l, 0.0)
    if extra_scale_lanes is not None:
        lo, hi, value = extra_scale_lanes
        full = jnp.arange(LANES)[None, :]
        c = jnp.where(full >= hi, 0.0, jnp.where(full >= lo, value, c))
        s_up = jnp.where(full >= lo, 0.0, s_up)
        s_dn = jnp.where(full >= lo, 0.0, s_dn)
    bc = lambda a: jnp.broadcast_to(a, (t, LANES)).astype(F32)
    return bc(c), bc(s_up), bc(s_dn)


def _tables(pos):
    wi_scale = float(N_IDX_HEADS * IDX_HEAD_DIM) ** -0.5
    return dict(
        qk=_rot_tables(pos, HEAD_DIM, ROT_DIM),
        idx=_rot_tables(pos, IDX_HEAD_DIM, IDX_ROT_DIM),
        kiwi=_rot_tables(pos, IDX_HEAD_DIM, IDX_ROT_DIM,
                         (IDX_HEAD_DIM, IDX_HEAD_DIM + N_IDX_HEADS, wi_scale)),
    )


TM_FFN, TF_FFN = 512, 512
TM_PROJ = 512
TM_POOL = 512
TQ_DSA, TK_DSA, TA_DSA = 256, 512, 512
TM_MERGE, TN_MERGE = 1024, 512


def kernel(x_prompt, x_sample, cache_k, cache_v, cache_kidx, state_pool, page_table,
           ffn1_norm, ffn1_w1, ffn1_w3, ffn1_w2, mix_norm, w_in, pool_group_w, pool_scale,
           w_up_pool, w_up_attn, w_out, ffn2_norm, ffn2_w1, ffn2_w3, ffn2_w2, final_norm):
    batch, seq, d = x_prompt.shape
    n_seq = x_sample.shape[0]
    depth = w_in.shape[0]

    bf = lambda a: a.astype(BF16)
    w_main = bf(w_in[..., :N_MAIN])
    w_gate = bf(w_in[..., N_MAIN + N_KIWI:])
    w_kiwi = jnp.pad(bf(w_in[..., N_MAIN:N_MAIN + N_KIWI]), ((0, 0), (0, 0), (0, LANES - N_KIWI)))
    f1 = (bf(ffn1_w1), bf(ffn1_w3), bf(ffn1_w2))
    f2 = (bf(ffn2_w1), bf(ffn2_w3), bf(ffn2_w2))
    wg, wp, wa, wo = bf(pool_group_w), bf(w_up_pool), bf(w_up_attn), bf(w_out)
    row = lambda a, l: a[l][None, :]

    tabs_p = _tables(jnp.arange(seq, dtype=F32))
    tabs_s = _tables(jnp.full((SAMPLE_ROWS,), PAST_LEN, F32))

    pad_s = SAMPLE_ROWS - n_seq
    state_t = jnp.pad(jnp.swapaxes(state_pool, 1, 2), ((0, 0), (0, 0), (0, pad_s), (0, 0)))
    kidx_t = jnp.swapaxes(cache_kidx, 2, 3)

    xp = x_prompt.reshape(batch * seq, d)
    xs = jnp.pad(x_sample.reshape(n_seq, d), ((0, pad_s), (0, 0)))
    fin = final_norm[None, :]
    outs = [[] for _ in range(8)]
    yp = ys = None
    for l in range(depth):
        xp, xn = ffn(xp, row(ffn1_norm, l), *f1, row(mix_norm, l), l, norm_dtype=BF16, tm=TM_FFN, tf=TF_FFN)
        u, q, k32, kbf, v32, qi, gl, kiwi, kdup, vt = proj(
            xn, w_main, w_gate, w_kiwi, l, tabs_p, seq=seq, tm=TM_PROJ, q_dtype=BF16, emit_vt=True)
        pool_y = pool_prompt(u, wg, row(pool_scale, l), l, batch=batch, seq=seq, tm=TM_POOL)
        attn_o = dsa_prompt(qi, kiwi, kdup, q, kbf, vt, batch=batch, seq=seq, tq=TQ_DSA, tk=TK_DSA, ta=TA_DSA)
        mm = merge(pool_y, attn_o, wp, wa, gl, l, tm=TM_MERGE, tn=TN_MERGE)
        xp = out_proj(mm, wo, xp, l, tm=TM_MERGE, tn=TN_MERGE)
        xp, yp = ffn(xp, row(ffn2_norm, l), *f2, fin, l, norm_dtype=F32, tm=TM_FFN, tf=TF_FFN)
        outs[0].append(k32.reshape(batch, seq, N_HEADS, HEAD_DIM))
        outs[1].append(v32.reshape(batch, seq, N_HEADS, HEAD_DIM))
        outs[2].append(kiwi[:, :IDX_HEAD_DIM].reshape(batch, seq, IDX_HEAD_DIM))
        outs[3].append(u.reshape(batch, seq, POOL_W)[:, seq - POOL_HIST:])

        xs, xn = ffn(xs, row(ffn1_norm, l), *f1, row(mix_norm, l), l, norm_dtype=BF16, tm=SAMPLE_ROWS, tf=TF_FFN)
        u, q, k32, _, v32, qi, gl, kiwi, _ = proj(
            xn, w_main, w_gate, w_kiwi, l, tabs_s, seq=SAMPLE_ROWS, tm=SAMPLE_ROWS, q_dtype=F32, emit_vt=False)
        pool_y = pool_sample(u, state_t[l], wg, row(pool_scale, l), l)
        qi3 = qi.reshape(SAMPLE_ROWS, N_IDX_HEADS, IDX_HEAD_DIM)
        wi3 = kiwi[:, IDX_HEAD_DIM:N_KIWI].reshape(SAMPLE_ROWS, N_IDX_HEADS, 1)
        scores = sample_scores(page_table, qi3, wi3, kidx_t, l, n_seq=n_seq)
        idx = sample_select(scores, qi3, wi3, kiwi.reshape(SAMPLE_ROWS, 1, LANES), n_seq=n_seq).reshape(n_seq, -1)
        r3 = lambda a: a.reshape(SAMPLE_ROWS, N_HEADS, HEAD_DIM)
        a3 = sample_attn(page_table, idx, r3(q), r3(k32), r3(v32), cache_k, cache_v, l, n_seq=n_seq)
        attn_o = jnp.pad(a3.reshape(n_seq, ATTN_W), ((0, pad_s), (0, 0))).astype(BF16)
        mm = merge(pool_y, attn_o, wp, wa, gl, l, tm=SAMPLE_ROWS, tn=TN_MERGE)
        xs = out_proj(mm, wo, xs, l, tm=SAMPLE_ROWS, tn=TN_MERGE)
        xs, ys = ffn(xs, row(ffn2_norm, l), *f2, fin, l, norm_dtype=F32, tm=SAMPLE_ROWS, tf=TF_FFN)
        outs[4].append(k32[:n_seq].reshape(n_seq, 1, N_HEADS, HEAD_DIM))
        outs[5].append(v32[:n_seq].reshape(n_seq, 1, N_HEADS, HEAD_DIM))
        outs[6].append(kiwi[:n_seq, :IDX_HEAD_DIM].reshape(n_seq, 1, IDX_HEAD_DIM))
        outs[7].append(jnp.concatenate([state_pool[l][:, 1:], u[:n_seq, None, :]], axis=1))

    y_prompt = yp.reshape(batch, seq, d)
    y_sample = ys[:n_seq].reshape(n_seq, 1, d)
    st = [jnp.stack(o) for o in outs]
    return (y_prompt, y_sample, st[0], st[1], st[2], st[3], st[4], st[5], st[6], st[7])
```

```python
import functools

import jax
import jax.numpy as jnp
from jax import lax
from jax.experimental import pallas as pl
from jax.experimental.pallas import tpu as pltpu

F32 = jnp.float32
BF16 = jnp.bfloat16
I32 = jnp.int32

D_MODEL = 2048
DEPTH = 4
PAST_LEN = 16384
PAGE_SIZE = 128
N_PAGES = PAST_LEN // PAGE_SIZE
N_HEADS = 8
HEAD_DIM = 128
ATTN_W = N_HEADS * HEAD_DIM
ROT_DIM = HEAD_DIM // 4
ROPE_THETA = 500000.0
N_IDX_HEADS = 16
IDX_HEAD_DIM = 64
IDX_W = N_IDX_HEADS * IDX_HEAD_DIM
IDX_ROT_DIM = IDX_HEAD_DIM // 4
TOPK_MAX = 256
POOL_WINDOWS = (2, 4, 8, 16)
POOL_GROUP_W = 256
POOL_W = len(POOL_WINDOWS) * POOL_GROUP_W
POOL_HIST = max(POOL_WINDOWS) - 1
D_FF = 5632
NORM_EPS = 1e-6

LANES = 128
SUBLANES = 8
HALO = 16
VMEM_LIMIT = 56 * 1024 * 1024
INT_MIN = -2 ** 31
NEG_BIG = -1e30
SAMPLE_ROWS = 16

N_MAIN = POOL_W + 3 * ATTN_W + IDX_W
N_KIWI = IDX_HEAD_DIM + N_IDX_HEADS


def _params(*sem):
    return pltpu.CompilerParams(dimension_semantics=sem, vmem_limit_bytes=VMEM_LIMIT)


def _rms(x, g):
    y = x * lax.rsqrt(jnp.mean(x * x, axis=-1, keepdims=True) + NORM_EPS)
    return y * g


def _dot(a, b):
    return jnp.dot(a, b, preferred_element_type=F32)


def _dot_nt(a, b):
    return lax.dot_general(a, b, (((1,), (1,)), ((), ())), preferred_element_type=F32)


def _ffn_kernel(x_ref, g_ref, w1_ref, w3_ref, w2_ref, g2_ref, o_ref, n_ref, xn_ref, acc_ref):
    f = pl.program_id(1)

    @pl.when(f == 0)
    def _():
        xn_ref[...] = _rms(x_ref[...], g_ref[...]).astype(BF16)
        acc_ref[...] = jnp.zeros_like(acc_ref)

    xn = xn_ref[...]
    a = _dot(xn, w1_ref[...])
    b = _dot(xn, w3_ref[...])
    h = (a * jax.nn.sigmoid(a)) * b
    acc_ref[...] += _dot(h.astype(BF16), w2_ref[...])

    @pl.when(f == pl.num_programs(1) - 1)
    def _():
        out = x_ref[...] + 0.5 * acc_ref[...]
        o_ref[...] = out
        n_ref[...] = _rms(out, g2_ref[...]).astype(n_ref.dtype)


def ffn(x, g, w1, w3, w2, g2, layer, *, norm_dtype, tm, tf):
    m, d = x.shape
    d_ff = w1.shape[2]
    return pl.pallas_call(
        _ffn_kernel,
        grid=(m // tm, d_ff // tf),
        in_specs=[
            pl.BlockSpec((tm, d), lambda i, f: (i, 0)),
            pl.BlockSpec((1, d), lambda i, f: (0, 0)),
            pl.BlockSpec((None, d, tf), lambda i, f: (layer, 0, f)),
            pl.BlockSpec((None, d, tf), lambda i, f: (layer, 0, f)),
            pl.BlockSpec((None, tf, d), lambda i, f: (layer, f, 0)),
            pl.BlockSpec((1, d), lambda i, f: (0, 0)),
        ],
        out_specs=[
            pl.BlockSpec((tm, d), lambda i, f: (i, 0)),
            pl.BlockSpec((tm, d), lambda i, f: (i, 0)),
        ],
        out_shape=[jax.ShapeDtypeStruct((m, d), F32), jax.ShapeDtypeStruct((m, d), norm_dtype)],
        scratch_shapes=[pltpu.VMEM((tm, d), BF16), pltpu.VMEM((tm, d), F32)],
        compiler_params=_params("parallel", "arbitrary"),
        name="ffn",
    )(x, g, w1, w3, w2, g2)


def _rotate(z, c, s_up, s_dn, half):
    outs = []
    for g in range(z.shape[1] // LANES):
        zg = z[:, g * LANES:(g + 1) * LANES]
        up = pltpu.roll(zg, LANES - half, 1)
        dn = pltpu.roll(zg, half, 1)
        outs.append(zg * c + up * s_up + dn * s_dn)
    return outs[0] if len(outs) == 1 else jnp.concatenate(outs, axis=1)


PROJ_ROWS = 256


def _proj_kernel(*refs, rot_half, n_out, dup, vt):
    a_ref, w_ref = refs[:2]
    outs = refs[5:] if rot_half else refs[2:]
    tm = a_ref.shape[0]
    step = min(tm, PROJ_ROWS)
    chunks = [slice(r, r + step) for r in range(0, tm, step)]
    w = w_ref[...]
    zs = [_dot_nt(a_ref[rs, :], w) for rs in chunks]
    for rs, z in zip(chunks, zs):
        if rot_half:
            c_ref, su_ref, sd_ref = refs[2:5]
            z = _rotate(z, c_ref[rs, :], su_ref[rs, :], sd_ref[rs, :], rot_half)
        for o_ref in outs[:n_out]:
            o_ref[rs, :] = z.astype(o_ref.dtype)
        if dup:
            lane = lax.broadcasted_iota(I32, z.shape, 1)
            zz = jnp.where(lane < IDX_HEAD_DIM, z, pltpu.roll(z, IDX_HEAD_DIM, 1))
            outs[n_out][rs, :] = zz.astype(BF16)
        if vt:
            outs[n_out][:, rs] = jnp.transpose(z).astype(BF16)


def proj(a, wt, layer, row0, n, out_dtypes, *, tm, tn, tables=None, rot_half=0, seq=None,
         dup=False, vt=False):
    m, k = a.shape
    tn = min(tn, n)
    rb = row0 // tn
    in_specs = [
        pl.BlockSpec((tm, k), lambda i, j: (i, 0)),
        pl.BlockSpec((None, tn, k), lambda i, j: (layer, rb + j, 0)),
    ]
    args = [a, wt]
    nt = seq // tm
    if rot_half:
        tspec = pl.BlockSpec((tm, LANES), lambda i, j: (i % nt, 0))
        in_specs += [tspec, tspec, tspec]
        args += list(tables)
    dts = list(out_dtypes) + ([BF16] if dup else [])
    out_specs = [pl.BlockSpec((tm, tn), lambda i, j: (i, j)) for _ in dts]
    out_shape = [jax.ShapeDtypeStruct((m, n), dt) for dt in dts]
    if vt:
        out_specs.append(pl.BlockSpec((None, tn, tm), lambda i, j: (i // nt, j, i % nt)))
        out_shape.append(jax.ShapeDtypeStruct((m // seq, n, seq), BF16))
    return pl.pallas_call(
        functools.partial(_proj_kernel, rot_half=rot_half, n_out=len(out_dtypes), dup=dup, vt=vt),
        grid=(m // tm, n // tn),
        in_specs=in_specs,
        out_specs=out_specs,
        out_shape=out_shape,
        compiler_params=_params("parallel", "arbitrary"),
        name="proj",
    )(*args)


def _pool_groups(buf_ref, u, cnt_fn, wg_ref, sc_ref, o_ref, tm):
    for g, w in enumerate(POOL_WINDOWS):
        lo, hi = g * POOL_GROUP_W, (g + 1) * POOL_GROUP_W
        s = buf_ref[HALO:HALO + tm, lo:hi]
        for j in range(1, w):
            s = s + buf_ref[HALO - j:HALO - j + tm, lo:hi]
        d = s / cnt_fn(w) - u[:, lo:hi]
        y = _dot(d.astype(BF16), wg_ref[g]) * sc_ref[:, lo:hi]
        o_ref[:, lo:hi] = y.astype(o_ref.dtype)


def _pool_prompt_kernel(u_ref, prev_ref, wg_ref, sc_ref, o_ref, buf_ref, *, tm):
    i = pl.program_id(1)
    u = u_ref[...]
    buf_ref[0:HALO, :] = jnp.where(i == 0, 0.0, prev_ref[...])
    buf_ref[HALO:HALO + tm, :] = u
    pos = (i * tm + lax.broadcasted_iota(I32, (tm, 1), 0)).astype(F32)
    _pool_groups(buf_ref, u, lambda w: jnp.minimum(pos + 1.0, float(w)), wg_ref, sc_ref, o_ref, tm)


def pool_prompt(u, wg, sc, layer, *, batch, seq, tm):
    nt = seq // tm
    hb = tm // HALO
    return pl.pallas_call(
        functools.partial(_pool_prompt_kernel, tm=tm),
        grid=(batch, nt),
        in_specs=[
            pl.BlockSpec((tm, POOL_W), lambda b, i: (b * nt + i, 0)),
            pl.BlockSpec((HALO, POOL_W), lambda b, i: (jnp.maximum((b * nt + i) * hb - 1, 0), 0)),
            pl.BlockSpec((None, len(POOL_WINDOWS), POOL_GROUP_W, POOL_GROUP_W), lambda b, i: (layer, 0, 0, 0)),
            pl.BlockSpec((1, POOL_W), lambda b, i: (0, 0)),
        ],
        out_specs=pl.BlockSpec((tm, POOL_W), lambda b, i: (b * nt + i, 0)),
        out_shape=jax.ShapeDtypeStruct((batch * seq, POOL_W), BF16),
        scratch_shapes=[pltpu.VMEM((HALO + tm, POOL_W), F32)],
        compiler_params=_params("parallel", "arbitrary"),
        name="pool_prompt",
    )(u, u, wg, sc)


def _pool_sample_kernel(u_ref, st_ref, wg_ref, sc_ref, o_ref):
    u = u_ref[...]
    for g, w in enumerate(POOL_WINDOWS):
        lo, hi = g * POOL_GROUP_W, (g + 1) * POOL_GROUP_W
        s = u[:, lo:hi]
        for j in range(1, w):
            s = s + st_ref[POOL_HIST - j, :, lo:hi]
        d = s / float(w) - u[:, lo:hi]
        y = _dot(d.astype(BF16), wg_ref[g]) * sc_ref[:, lo:hi]
        o_ref[:, lo:hi] = y.astype(o_ref.dtype)


def pool_sample(u, state_t, wg, sc, layer):
    rows = u.shape[0]
    return pl.pallas_call(
        _pool_sample_kernel,
        grid=(1,),
        in_specs=[
            pl.BlockSpec((rows, POOL_W), lambda i: (0, 0)),
            pl.BlockSpec((POOL_HIST, rows, POOL_W), lambda i: (0, 0, 0)),
            pl.BlockSpec((None, len(POOL_WINDOWS), POOL_GROUP_W, POOL_GROUP_W), lambda i: (layer, 0, 0, 0)),
            pl.BlockSpec((1, POOL_W), lambda i: (0, 0)),
        ],
        out_specs=pl.BlockSpec((rows, POOL_W), lambda i: (0, 0)),
        out_shape=jax.ShapeDtypeStruct((rows, POOL_W), BF16),
        compiler_params=_params("arbitrary"),
        name="pool_sample",
    )(u, state_t, wg, sc)


def _sortable(score):
    bits = lax.bitcast_convert_type(score, I32)
    return jnp.where(bits >= 0, bits, bits ^ jnp.int32(0x7FFFFFFF))


def _search_bits(count_fn, shape, nbits, accept, count0):
    def step(i, carry):
        acc, cnt_acc = carry
        bit = jnp.left_shift(jnp.int32(1), (nbits - 1) - i)
        cand = acc | bit
        cnt = count_fn(cand)
        ok = accept(cnt)
        return jnp.where(ok, cand, acc), jnp.where(ok, cnt, cnt_acc)
    init = (jnp.zeros(shape, I32), jnp.broadcast_to(count0, shape).astype(I32))
    return lax.fori_loop(0, nbits, step, init)


COUNT_SUMS = 8


def _dsa_prompt_kernel(qi_ref, kw_ref, kd_ref, q_ref, k_ref, vt_ref, o_ref,
                       qm_ref, key_ref, bias_ref, acc_ref, m_ref, l_ref, *, tq, tk, ta, topk, pos_bits):
    qb = pl.program_id(1)
    n_chunks = (qb * tq + tq + tk - 1) // tk
    q_pos = qb * tq + lax.broadcasted_iota(I32, (1, tq), 1)
    lane = lax.broadcasted_iota(I32, (tq, LANES), 1)

    for hp in range(N_IDX_HEADS // 2):
        pair = qi_ref[:, hp * LANES:(hp + 1) * LANES]
        qm_ref[2 * hp] = jnp.where(lane < IDX_HEAD_DIM, pair, jnp.zeros_like(pair))
        qm_ref[2 * hp + 1] = jnp.where(lane >= IDX_HEAD_DIM, pair, jnp.zeros_like(pair))
    wi_t = jnp.transpose(kw_ref[...])

    def key_pos(c):
        return c * tk + lax.broadcasted_iota(I32, (tk, 1), 0)

    def chunk(c):
        return pl.ds(pl.multiple_of(c * tk, tk), tk)

    def score_chunk(c, carry):
        kc = kd_ref[chunk(c), :]
        acc = jnp.zeros((tk, tq), F32)
        for h in range(N_IDX_HEADS):
            s = _dot_nt(kc, qm_ref[h])
            acc = acc + jnp.maximum(s, 0.0) * wi_t[IDX_HEAD_DIM + h:IDX_HEAD_DIM + h + 1, :]
        keys = jnp.where(key_pos(c) <= q_pos, _sortable(acc), jnp.int32(INT_MIN))
        key_ref[chunk(c), :] = keys
        return carry
    lax.fori_loop(0, n_chunks, score_chunk, 0)

    def count(pred_fn):
        def body(c, accs):
            part = jnp.where(pred_fn(key_ref[chunk(c), :], c), 1, 0)
            accs = list(accs)
            for r in range(tk // SUBLANES):
                a = r % len(accs)
                accs[a] = accs[a] + part[r * SUBLANES:(r + 1) * SUBLANES, :]
            return tuple(accs)
        zero = jnp.zeros((SUBLANES, tq), I32)
        accs = lax.fori_loop(0, n_chunks, body, (zero,) * COUNT_SUMS)
        return jnp.sum(sum(accs[1:], accs[0]), axis=0, keepdims=True)

    biased, n_ge = _search_bits(
        lambda cand: count(lambda k, c: k >= (cand ^ jnp.int32(INT_MIN))),
        (1, tq), 32, lambda cnt: cnt >= topk, n_chunks * tk)
    thr = biased ^ jnp.int32(INT_MIN)

    tied = (n_ge > topk) & (thr != jnp.int32(INT_MIN))

    def break_ties():
        need = topk - count(lambda k, c: k > thr)
        jm, _ = _search_bits(
            lambda cand: count(lambda k, c: (k == thr) & (key_pos(c) < cand)),
            (1, tq), pos_bits, lambda cnt: cnt < need, 0)
        return jm

    jmax = lax.cond(jnp.max(jnp.where(tied, 1, 0)) > 0, break_ties,
                    lambda: jnp.full((1, tq), 2 ** 31 - 1, I32))

    def bias_chunk(c, carry):
        k = key_ref[chunk(c), :]
        kp = key_pos(c)
        sel = ((k > thr) | ((k == thr) & (kp <= jmax))) & (kp <= q_pos)
        bias_ref[chunk(c), :] = jnp.where(sel, 0.0, NEG_BIG)
        return carry
    lax.fori_loop(0, n_chunks, bias_chunk, 0)

    scale = HEAD_DIM ** -0.5
    acc_ref[...] = jnp.zeros_like(acc_ref)
    m_ref[...] = jnp.full(m_ref.shape, NEG_BIG, F32)
    l_ref[...] = jnp.zeros_like(l_ref)

    def attend(c, carry):
        rows = pl.ds(pl.multiple_of(c * ta, ta), ta)
        heads = [slice(h * HEAD_DIM, (h + 1) * HEAD_DIM) for h in range(N_HEADS)]
        raw = [_dot_nt(k_ref[rows, hs], q_ref[:, hs]) for hs in heads]
        bias = bias_ref[rows, :]
        probs, alphas = [], []
        for h in range(N_HEADS):
            s = raw[h] * scale + bias
            m_old = m_ref[h, 0:1, :]
            m_new = jnp.maximum(m_old, jnp.max(s, axis=0, keepdims=True))
            alpha = jnp.exp(m_old - m_new)
            p = jnp.exp(s - m_new)
            l_new = alpha * l_ref[h, 0:1, :] + jnp.sum(p, axis=0, keepdims=True)
            m_ref[h] = jnp.broadcast_to(m_new, (SUBLANES, tq))
            l_ref[h] = jnp.broadcast_to(l_new, (SUBLANES, tq))
            probs.append(p.astype(BF16))
            alphas.append(alpha)
        for h, hs in enumerate(heads):
            acc_ref[hs, :] = alphas[h] * acc_ref[hs, :] + _dot(vt_ref[hs, rows], probs[h])
        return carry
    lax.fori_loop(0, (qb * tq + tq + ta - 1) // ta, attend, 0)
    for h in range(N_HEADS):
        hs = slice(h * HEAD_DIM, (h + 1) * HEAD_DIM)
        o_ref[:, hs] = jnp.transpose(acc_ref[hs, :] / l_ref[h, 0:1, :]).astype(o_ref.dtype)


def dsa_prompt(qi, kiwi, kdup, q, k, vt, *, batch, seq, tq, tk, ta):
    nq = seq // tq
    topk = min(TOPK_MAX, seq // 4)
    row = lambda b, i: (b * nq + i, 0)
    whole = lambda b, i: (b, 0)
    return pl.pallas_call(
        functools.partial(_dsa_prompt_kernel, tq=tq, tk=tk, ta=ta, topk=topk, pos_bits=(seq - 1).bit_length()),
        grid=(batch, nq),
        in_specs=[
            pl.BlockSpec((tq, IDX_W), row),
            pl.BlockSpec((tq, LANES), row),
            pl.BlockSpec((seq, LANES), whole),
            pl.BlockSpec((tq, ATTN_W), row),
            pl.BlockSpec((seq, ATTN_W), whole),
            pl.BlockSpec((None, ATTN_W, seq), lambda b, i: (b, 0, 0)),
        ],
        out_specs=pl.BlockSpec((tq, ATTN_W), row),
        out_shape=jax.ShapeDtypeStruct((batch * seq, ATTN_W), BF16),
        scratch_shapes=[
            pltpu.VMEM((N_IDX_HEADS, tq, LANES), BF16),
            pltpu.VMEM((seq, tq), I32),
            pltpu.VMEM((seq, tq), F32),
            pltpu.VMEM((ATTN_W, tq), F32),
            pltpu.VMEM((N_HEADS, SUBLANES, tq), F32),
            pltpu.VMEM((N_HEADS, SUBLANES, tq), F32),
        ],
        compiler_params=_params("parallel", "arbitrary"),
        name="dsa_prompt",
    )(qi, kiwi, kdup, q, k, vt)


def _merge_kernel(p_ref, a_ref, wp_ref, wa_ref, gp_ref, ga_ref, o_ref):
    yp = _dot(p_ref[...], wp_ref[...])
    ya = _dot(a_ref[...], wa_ref[...])
    m = jax.nn.sigmoid(gp_ref[...]) * yp + jax.nn.sigmoid(ga_ref[...]) * ya
    o_ref[...] = m.astype(o_ref.dtype)


def merge(pool_y, attn_o, wp, wa, gl, layer, *, tm, tn):
    m, kp = pool_y.shape
    ka = attn_o.shape[1]
    nj = D_MODEL // tn
    return pl.pallas_call(
        _merge_kernel,
        grid=(m // tm, nj),
        in_specs=[
            pl.BlockSpec((tm, kp), lambda i, j: (i, 0)),
            pl.BlockSpec((tm, ka), lambda i, j: (i, 0)),
            pl.BlockSpec((None, kp, tn), lambda i, j: (layer, 0, j)),
            pl.BlockSpec((None, ka, tn), lambda i, j: (layer, 0, j)),
            pl.BlockSpec((tm, tn), lambda i, j: (i, j)),
            pl.BlockSpec((tm, tn), lambda i, j: (i, nj + j)),
        ],
        out_specs=pl.BlockSpec((tm, tn), lambda i, j: (i, j)),
        out_shape=jax.ShapeDtypeStruct((m, D_MODEL), BF16),
        compiler_params=_params("parallel", "arbitrary"),
        name="merge",
    )(pool_y, attn_o, wp, wa, gl, gl)


def _out_kernel(m_ref, w_ref, x_ref, o_ref):
    o_ref[...] = x_ref[...] + _dot(m_ref[...], w_ref[...])


def out_proj(mm, w, x, layer, *, tm, tn):
    m, k = mm.shape
    return pl.pallas_call(
        _out_kernel,
        grid=(m // tm, D_MODEL // tn),
        in_specs=[
            pl.BlockSpec((tm, k), lambda i, j: (i, 0)),
            pl.BlockSpec((None, k, tn), lambda i, j: (layer, 0, j)),
            pl.BlockSpec((tm, tn), lambda i, j: (i, j)),
        ],
        out_specs=pl.BlockSpec((tm, tn), lambda i, j: (i, j)),
        out_shape=jax.ShapeDtypeStruct((m, D_MODEL), F32),
        compiler_params=_params("parallel", "arbitrary"),
        name="out_proj",
    )(mm, w, x)


IDX_PG = 16
assert N_PAGES == LANES


def _sample_score_kernel(pt_ref, q_ref, w_ref, *refs):
    del pt_ref
    pages = refs[:IDX_PG]
    o_ref = refs[IDX_PG]
    q16 = q_ref[...]
    wcol = w_ref[:, 0:1]
    for p in range(IDX_PG):
        kp = pages[p][...].astype(BF16)
        s = jnp.maximum(_dot(q16, kp), 0.0) * wcol
        o_ref[p:p + 1, :] = jnp.sum(s, axis=0, keepdims=True)


def sample_scores(page_table, qi3, wi3, cache_kidx_t, layer, *, n_seq):
    def page_spec(p):
        return pl.BlockSpec((None, None, IDX_HEAD_DIM, PAGE_SIZE),
                            lambda b, g, pt: (layer, pt[b, g * IDX_PG + p], 0, 0))
    grid_spec = pltpu.PrefetchScalarGridSpec(
        num_scalar_prefetch=1,
        grid=(n_seq, N_PAGES // IDX_PG),
        in_specs=[
            pl.BlockSpec((None, N_IDX_HEADS, IDX_HEAD_DIM), lambda b, g, pt: (b, 0, 0)),
            pl.BlockSpec((None, N_IDX_HEADS, 1), lambda b, g, pt: (b, 0, 0)),
        ] + [page_spec(p) for p in range(IDX_PG)],
        out_specs=pl.BlockSpec((None, IDX_PG, LANES), lambda b, g, pt: (b, g, 0)),
    )
    return pl.pallas_call(
        _sample_score_kernel,
        grid_spec=grid_spec,
        out_shape=jax.ShapeDtypeStruct((n_seq, N_PAGES, LANES), F32),
        compiler_params=_params("parallel", "arbitrary"),
        name="sample_scores",
    )(page_table, qi3, wi3, *([cache_kidx_t] * IDX_PG))


def _sample_select_kernel(sc_ref, q_ref, w_ref, kn_ref, idx_ref, rank_ref, *, topk):
    n_seq = sc_ref.shape[0]
    lane = lax.broadcasted_iota(I32, (N_PAGES, LANES), 1)
    page = lax.broadcasted_iota(I32, (N_PAGES, LANES), 0)
    pos = page * PAGE_SIZE + lane
    before = jnp.where(page < lane, 1.0, 0.0).astype(BF16)
    after = jnp.where(lane < page, 1.0, 0.0).astype(BF16)
    slot = lax.broadcasted_iota(I32, (topk, 1), 0)

    keys = _sortable(sc_ref[...])
    kn = kn_ref[0:n_seq, :, 0:IDX_HEAD_DIM].astype(BF16).astype(F32)
    dh = jnp.sum(q_ref[0:n_seq].astype(F32) * kn, axis=2, keepdims=True)
    s_new = jnp.sum(jnp.maximum(dh, 0.0) * w_ref[0:n_seq], axis=1, keepdims=True)
    key_new = _sortable(s_new)
    state = (n_seq, 1, 1)

    def count(pred_cached, pred_new):
        c = jnp.sum(jnp.where(pred_cached, 1, 0), axis=2, keepdims=True)
        c = jnp.sum(c, axis=1, keepdims=True)
        return c + jnp.where(pred_new, 1, 0)

    def ge(cand):
        t = cand ^ jnp.int32(INT_MIN)
        return count(keys >= t, key_new >= t)

    biased, _ = _search_bits(ge, state, 32, lambda cnt: cnt >= topk, PAST_LEN + 1)
    thr = biased ^ jnp.int32(INT_MIN)
    need = topk - count(keys > thr, key_new > thr)

    def eq_before(cand):
        return count((keys == thr) & (pos[None] < cand), (key_new == thr) & (PAST_LEN < cand))

    jmax, _ = _search_bits(eq_before, state, PAST_LEN.bit_length(), lambda cnt: cnt < need, 0)
    sel_all = (keys > thr) | ((keys == thr) & (pos[None] <= jmax))
    sel_new_all = (key_new > thr) | ((key_new == thr) & (PAST_LEN <= jmax))

    for b in range(n_seq):
        sel = sel_all[b]
        sel_new = sel_new_all[b]
        sel_f = jnp.where(sel, 1.0, 0.0)
        in_row = _dot(sel_f.astype(BF16), before)
        tot = jnp.broadcast_to(jnp.sum(sel_f, axis=1, keepdims=True), (N_PAGES, LANES))
        rows_before = _dot(after, tot.astype(BF16))
        rank_ref[...] = jnp.where(sel, (in_row + rows_before).astype(I32), -1)

        def place(p, acc):
            hit = rank_ref[pl.ds(p, 1), :] == slot
            return acc + jnp.where(hit, p * PAGE_SIZE + lane[0:1, :], 0)

        acc = lax.fori_loop(0, N_PAGES, place, jnp.zeros((topk, LANES), I32))
        idx = jnp.sum(acc, axis=1, keepdims=True)
        idx_ref[b] = jnp.where(sel_new & (slot == topk - 1), PAST_LEN, idx)


def sample_select(scores, qi3, wi3, kiwi, *, n_seq):
    topk = min(TOPK_MAX, (PAST_LEN + 1) // 4)
    full = lambda a: pl.BlockSpec(a.shape, lambda i: (0,) * a.ndim)
    return pl.pallas_call(
        functools.partial(_sample_select_kernel, topk=topk),
        grid=(1,),
        in_specs=[full(scores), full(qi3), full(wi3), full(kiwi)],
        out_specs=pl.BlockSpec((n_seq, topk, 1), lambda i: (0, 0, 0)),
        out_shape=jax.ShapeDtypeStruct((n_seq, topk, 1), I32),
        scratch_shapes=[pltpu.VMEM((N_PAGES, LANES), I32)],
        compiler_params=_params("arbitrary"),
        name="sample_select",
    )(scores, qi3, wi3, kiwi)


def _sample_attn_kernel(pt_ref, idx_ref, q_ref, kn_ref, vn_ref, ck_ref, cv_ref, o_ref,
                        kbuf, vbuf, sem, *, layer, topk):
    b = pl.program_id(0)

    def copies(i):
        cached = jnp.minimum(idx_ref[b, i], PAST_LEN - 1)
        page = pt_ref[b, lax.shift_right_logical(cached, PAGE_SIZE.bit_length() - 1)]
        off = cached & (PAGE_SIZE - 1)
        return (pltpu.make_async_copy(ck_ref.at[layer, page, off], kbuf.at[i], sem.at[0]),
                pltpu.make_async_copy(cv_ref.at[layer, page, off], vbuf.at[i], sem.at[1]))

    def start(i, carry):
        for cp in copies(i):
            cp.start()
        return carry

    def wait(i, carry):
        for cp in copies(i):
            cp.wait()
        return carry

    lax.fori_loop(0, topk, start, 0)
    lax.fori_loop(0, topk, wait, 0)

    @pl.when(idx_ref[b, topk - 1] >= PAST_LEN)
    def _():
        kbuf[topk - 1] = kn_ref[...]
        vbuf[topk - 1] = vn_ref[...]

    rb = lambda a: a.astype(BF16).astype(F32)
    q = rb(q_ref[...])
    s = jnp.sum(rb(kbuf[...]) * q[None], axis=-1, keepdims=True) * (HEAD_DIM ** -0.5)
    m = jnp.max(s, axis=0, keepdims=True)
    p = jnp.exp(s - m)
    l = jnp.sum(p, axis=0)
    o_ref[...] = jnp.sum(rb(p) * rb(vbuf[...]), axis=0) / l


def sample_attn(page_table, idx, q3, kn3, vn3, cache_k, cache_v, layer, *, n_seq):
    topk = idx.shape[1]
    tile = pl.BlockSpec((None, N_HEADS, HEAD_DIM), lambda b, pt, ix: (b, 0, 0))
    grid_spec = pltpu.PrefetchScalarGridSpec(
        num_scalar_prefetch=2,
        grid=(n_seq,),
        in_specs=[tile, tile, tile, pl.BlockSpec(memory_space=pl.ANY), pl.BlockSpec(memory_space=pl.ANY)],
        out_specs=tile,
        scratch_shapes=[
            pltpu.VMEM((topk, N_HEADS, HEAD_DIM), F32),
            pltpu.VMEM((topk, N_HEADS, HEAD_DIM), F32),
            pltpu.SemaphoreType.DMA((2,)),
        ],
    )
    return pl.pallas_call(
        functools.partial(_sample_attn_kernel, layer=layer, topk=topk),
        grid_spec=grid_spec,
        out_shape=jax.ShapeDtypeStruct((n_seq, N_HEADS, HEAD_DIM), F32),
        compiler_params=_params("arbitrary"),
        name="sample_attn",
    )(page_table, idx, q3, kn3, vn3, cache_k, cache_v)


def _rot_tables(pos, head_dim, rot_dim, extra_scale_lanes=None):
    half = rot_dim // 2
    inv_freq = jnp.float32(ROPE_THETA) ** (-jnp.arange(half, dtype=F32) * (2.0 / rot_dim))
    ang = pos[:, None] * inv_freq[None, :]
    cos, sin = jnp.cos(ang), jnp.sin(ang)
    t = pos.shape[0]
    lane = jnp.arange(LANES) % head_dim
    cos_l = jnp.take(cos, lane % half, axis=1)
    sin_l = jnp.take(sin, lane % half, axis=1)
    first = (lane < half)[None, :]
    second = ((lane >= half) & (lane < rot_dim))[None, :]
    c = jnp.where(first | second, cos_l, 1.0)
    s_up = jnp.where(first, -sin_l, 0.0)
    s_dn = jnp.where(second, sin_l, 0.0)
    if extra_scale_lanes is not None:
        lo, hi, value = extra_scale_lanes
        full = jnp.arange(LANES)[None, :]
        c = jnp.where(full >= hi, 0.0, jnp.where(full >= lo, value, c))
        s_up = jnp.where(full >= lo, 0.0, s_up)
        s_dn = jnp.where(full >= lo, 0.0, s_dn)
    bc = lambda a: jnp.broadcast_to(a, (t, LANES)).astype(F32)
    return bc(c), bc(s_up), bc(s_dn)


def _tables(pos):
    wi_scale = float(N_IDX_HEADS * IDX_HEAD_DIM) ** -0.5
    return dict(
        qk=_rot_tables(pos, HEAD_DIM, ROT_DIM),
        idx=_rot_tables(pos, IDX_HEAD_DIM, IDX_ROT_DIM),
        kiwi=_rot_tables(pos, IDX_HEAD_DIM, IDX_ROT_DIM,
                         (IDX_HEAD_DIM, IDX_HEAD_DIM + N_IDX_HEADS, wi_scale)),
    )


def _project(xn, w_main_t, w_gate_t, w_kiwi_t, layer, tabs, *, seq, tm, tn, q_dtype, vt):
    kw = dict(tm=tm, tn=tn, seq=seq)
    qk = dict(tables=tabs["qk"], rot_half=ROT_DIM // 2, **kw)
    (u,) = proj(xn, w_main_t, layer, 0, POOL_W, [F32], **kw)
    (q,) = proj(xn, w_main_t, layer, POOL_W, ATTN_W, [q_dtype], **qk)
    k32, kbf = proj(xn, w_main_t, layer, POOL_W + ATTN_W, ATTN_W, [F32, BF16], **qk)
    v = proj(xn, w_main_t, layer, POOL_W + 2 * ATTN_W, ATTN_W, [F32], vt=vt, **kw)
    (qi,) = proj(xn, w_main_t, layer, POOL_W + 3 * ATTN_W, IDX_W, [BF16],
                 tables=tabs["idx"], rot_half=IDX_ROT_DIM // 2, **kw)
    (gl,) = proj(xn, w_gate_t, layer, 0, 2 * D_MODEL, [F32], **kw)
    kiwi, kdup = proj(xn, w_kiwi_t, layer, 0, LANES, [F32], tables=tabs["kiwi"],
                      rot_half=IDX_ROT_DIM // 2, dup=True, **kw)
    return u, q, k32, kbf, v, qi, gl, kiwi, kdup


TM_FFN, TF_FFN = 512, 512
TM_PROJ, TN_PROJ = 1024, 512
TM_POOL = 512
TQ_DSA, TK_DSA, TA_DSA = 256, 512, 512
TM_MERGE, TN_MERGE = 1024, 512


def kernel(x_prompt, x_sample, cache_k, cache_v, cache_kidx, state_pool, page_table,
           ffn1_norm, ffn1_w1, ffn1_w3, ffn1_w2, mix_norm, w_in, pool_group_w, pool_scale,
           w_up_pool, w_up_attn, w_out, ffn2_norm, ffn2_w1, ffn2_w3, ffn2_w2, final_norm):
    batch, seq, d = x_prompt.shape
    n_seq = x_sample.shape[0]
    depth = w_in.shape[0]

    bf = lambda a: a.astype(BF16)
    w_in_t = jnp.swapaxes(w_in, 1, 2)
    w_main_t = bf(w_in_t[:, :N_MAIN])
    w_gate_t = bf(w_in_t[:, N_MAIN + N_KIWI:])
    w_kiwi_t = jnp.pad(bf(w_in_t[:, N_MAIN:N_MAIN + N_KIWI]), ((0, 0), (0, LANES - N_KIWI), (0, 0)))
    w_proj = (w_main_t, w_gate_t, w_kiwi_t)
    f1 = (bf(ffn1_w1), bf(ffn1_w3), bf(ffn1_w2))
    f2 = (bf(ffn2_w1), bf(ffn2_w3), bf(ffn2_w2))
    wg, wp, wa, wo = bf(pool_group_w), bf(w_up_pool), bf(w_up_attn), bf(w_out)
    row = lambda a, l: a[l][None, :]

    tabs_p = _tables(jnp.arange(seq, dtype=F32))
    tabs_s = _tables(jnp.full((SAMPLE_ROWS,), PAST_LEN, F32))

    pad_s = SAMPLE_ROWS - n_seq
    state_t = jnp.pad(jnp.swapaxes(state_pool, 1, 2), ((0, 0), (0, 0), (0, pad_s), (0, 0)))
    kidx_t = jnp.swapaxes(cache_kidx, 2, 3)

    xp = x_prompt.reshape(batch * seq, d)
    xs = jnp.pad(x_sample.reshape(n_seq, d), ((0, pad_s), (0, 0)))
    fin = final_norm[None, :]
    outs = [[] for _ in range(8)]
    yp = ys = None
    for l in range(depth):
        xp, xn = ffn(xp, row(ffn1_norm, l), *f1, row(mix_norm, l), l, norm_dtype=BF16, tm=TM_FFN, tf=TF_FFN)
        u, q, k32, kbf, (v32, vt), qi, gl, kiwi, kdup = _project(
            xn, *w_proj, l, tabs_p, seq=seq, tm=TM_PROJ, tn=TN_PROJ, q_dtype=BF16, vt=True)
        pool_y = pool_prompt(u, wg, row(pool_scale, l), l, batch=batch, seq=seq, tm=TM_POOL)
        attn_o = dsa_prompt(qi, kiwi, kdup, q, kbf, vt, batch=batch, seq=seq, tq=TQ_DSA, tk=TK_DSA, ta=TA_DSA)
        mm = merge(pool_y, attn_o, wp, wa, gl, l, tm=TM_MERGE, tn=TN_MERGE)
        xp = out_proj(mm, wo, xp, l, tm=TM_MERGE, tn=TN_MERGE)
        xp, yp = ffn(xp, row(ffn2_norm, l), *f2, fin, l, norm_dtype=F32, tm=TM_FFN, tf=TF_FFN)
        outs[0].append(k32.reshape(batch, seq, N_HEADS, HEAD_DIM))
        outs[1].append(v32.reshape(batch, seq, N_HEADS, HEAD_DIM))
        outs[2].append(kiwi[:, :IDX_HEAD_DIM].reshape(batch, seq, IDX_HEAD_DIM))
        outs[3].append(u.reshape(batch, seq, POOL_W)[:, seq - POOL_HIST:])

        xs, xn = ffn(xs, row(ffn1_norm, l), *f1, row(mix_norm, l), l, norm_dtype=BF16, tm=SAMPLE_ROWS, tf=TF_FFN)
        u, q, k32, _, (v32,), qi, gl, kiwi, _ = _project(
            xn, *w_proj, l, tabs_s, seq=SAMPLE_ROWS, tm=SAMPLE_ROWS, tn=TN_PROJ, q_dtype=F32, vt=False)
        pool_y = pool_sample(u, state_t[l], wg, row(pool_scale, l), l)
        qi3 = qi.reshape(SAMPLE_ROWS, N_IDX_HEADS, IDX_HEAD_DIM)
        wi3 = kiwi[:, IDX_HEAD_DIM:N_KIWI].reshape(SAMPLE_ROWS, N_IDX_HEADS, 1)
        scores = sample_scores(page_table, qi3, wi3, kidx_t, l, n_seq=n_seq)
        idx = sample_select(scores, qi3, wi3, kiwi.reshape(SAMPLE_ROWS, 1, LANES), n_seq=n_seq).reshape(n_seq, -1)
        r3 = lambda a: a.reshape(SAMPLE_ROWS, N_HEADS, HEAD_DIM)
        a3 = sample_attn(page_table, idx, r3(q), r3(k32), r3(v32), cache_k, cache_v, l, n_seq=n_seq)
        attn_o = jnp.pad(a3.reshape(n_seq, ATTN_W), ((0, pad_s), (0, 0))).astype(BF16)
        mm = merge(pool_y, attn_o, wp, wa, gl, l, tm=SAMPLE_ROWS, tn=TN_MERGE)
        xs = out_proj(mm, wo, xs, l, tm=SAMPLE_ROWS, tn=TN_MERGE)
        xs, ys = ffn(xs, row(ffn2_norm, l), *f2, fin, l, norm_dtype=F32, tm=SAMPLE_ROWS, tf=TF_FFN)
        outs[4].append(k32[:n_seq].reshape(n_seq, 1, N_HEADS, HEAD_DIM))
        outs[5].append(v32[:n_seq].reshape(n_seq, 1, N_HEADS, HEAD_DIM))
        outs[6].append(kiwi[:n_seq, :IDX_HEAD_DIM].reshape(n_seq, 1, IDX_HEAD_DIM))
        outs[7].append(jnp.concatenate([state_pool[l][:, 1:], u[:n_seq, None, :]], axis=1))

    y_prompt = yp.reshape(batch, seq, d)
    y_sample = ys[:n_seq].reshape(n_seq, 1, d)
    st = [jnp.stack(o) for o in outs]
    return (y_prompt, y_sample, st[0], st[1], st[2], st[3], st[4], st[5], st[6], st[7])
```
